```python
import math
import jax, jax.numpy as jnp
from jax import lax
import numpy as np

D_MODEL = 1024
BATCH = 4
SEQ = 4096
DEPTH = 1
DEC_BATCH = 128
DEC_SEQ = 1
PAST_LEN = 2048
PAGE_SIZE = 128

N_HEADS = 8
HEAD_DIM = 64
D_ATTN = N_HEADS * 2 * HEAD_DIM
POOL_WINDOWS = (2, 4, 8, 16)
N_POOL_GROUPS = len(POOL_WINDOWS)
D_POOL = D_MODEL // 2
POOL_GROUP = D_POOL // N_POOL_GROUPS
POOL_BUF = max(POOL_WINDOWS) - 1
D_FF = 4 * D_MODEL
ROPE_THETA = 10000.0
Q_BLOCK = 128
RMS_EPS = 1e-6
SUBLN_EPS = 1e-5
NEG_INF = -1e30
OFF_Q = D_POOL
OFF_K = OFF_Q + D_ATTN
OFF_V = OFF_K + D_ATTN
OFF_G = OFF_V + D_ATTN
D_IN = OFF_G + 2 * D_MODEL

kernel_name = 'hybrid_pool_diffattn_decode_step'


def _lambda_init(layer):
    return 0.8 - 0.6 * math.exp(-0.3 * layer)


def _rms_norm(x, g, eps):
    xf = x.astype(jnp.float32)
    xf = xf * lax.rsqrt(jnp.mean(xf * xf, axis=-1, keepdims=True) + eps)
    return (xf * g.astype(jnp.float32)).astype(x.dtype)


def _rope(x, pos):
    half = HEAD_DIM // 2
    inv_freq = ROPE_THETA ** (-jnp.arange(half, dtype=jnp.float32) * (2.0 / HEAD_DIM))
    ang = pos.astype(jnp.float32)[:, None] * inv_freq[None, :]
    cos = jnp.cos(ang)[:, None, None, :]
    sin = jnp.sin(ang)[:, None, None, :]
    xf = x.astype(jnp.float32)
    x1, x2 = xf[..., :half], xf[..., half:]
    return jnp.concatenate([x1 * cos - x2 * sin, x1 * sin + x2 * cos], axis=-1).astype(x.dtype)


def _pool_mixer(u_full, pos, w_grp, scale):
    B, L, _ = u_full.shape
    T = pos.shape[0]
    uf = u_full.astype(jnp.float32)
    cz = jnp.concatenate([jnp.zeros((B, 1, D_POOL), jnp.float32), jnp.cumsum(uf, axis=1)], axis=1)
    outs = []
    for g, w in enumerate(POOL_WINDOWS):
        sl = slice(g * POOL_GROUP, (g + 1) * POOL_GROUP)
        cg = cz[..., sl]
        c_lag = jnp.concatenate([jnp.zeros((B, w, POOL_GROUP), jnp.float32), cg], axis=1)
        win = cg[:, L + 1 - T:] - c_lag[:, L + 1 - T:L + 1]
        count = jnp.minimum(pos + 1, w).astype(jnp.float32)[None, :, None]
        outs.append(win / count - uf[:, L - T:, sl])
    pooled = jnp.stack(outs, axis=2).astype(u_full.dtype)
    y = jnp.einsum('btgc,gcd->btgd', pooled, w_grp).reshape(B, T, D_POOL)
    return y * scale


def _diff_attend(q, k, v, mask, lam):
    s = jnp.einsum('bqhcd,bkhcd->bhcqk', q, k).astype(jnp.float32) * (HEAD_DIM ** -0.5)
    s = jnp.where(mask[None, None, None], s, NEG_INF)
    p = jax.nn.softmax(s, axis=-1)
    a = p[:, :, 0] - lam * p[:, :, 1]
    return jnp.einsum('bhqk,bkhe->bqhe', a.astype(v.dtype), v)


def _attend_prompt(q, k, v, lam):
    B, T = q.shape[:2]
    nblk = T // Q_BLOCK
    qb = q.reshape(B, nblk, Q_BLOCK, N_HEADS, 2, HEAD_DIM).swapaxes(0, 1)
    kpos = jnp.arange(T)

    def block(args):
        qi, i = args
        qpos = i * Q_BLOCK + jnp.arange(Q_BLOCK)
        return _diff_attend(qi, k, v, kpos[None, :] <= qpos[:, None], lam)

    o = lax.map(block, (qb, jnp.arange(nblk)))
    return o.swapaxes(0, 1).reshape(B, T, N_HEADS, 2 * HEAD_DIM)


def _attend_sample(q, k_new, v_new, cache_k, cache_v, page_table, layer, lam):
    Bd, T = q.shape[:2]
    past = page_table.shape[1] * PAGE_SIZE
    pk = cache_k[layer, page_table].reshape(Bd, past, N_HEADS, 2, HEAD_DIM)
    pv = cache_v[layer, page_table].reshape(Bd, past, N_HEADS, 2 * HEAD_DIM)
    k_all = jnp.concatenate([pk.astype(k_new.dtype), k_new], axis=1)
    v_all = jnp.concatenate([pv.astype(v_new.dtype), v_new], axis=1)
    mask = jnp.arange(past + T)[None, :] <= (past + jnp.arange(T))[:, None]
    return _diff_attend(q, k_all, v_all, mask, lam)


def _decoder_layer(x, pos, pool_prev, cache_k, cache_v, page_table, layer, p):
    B, T, _ = x.shape
    h = _rms_norm(x, p['g_mix'], RMS_EPS)
    z = h @ p['w_in']
    u = z[..., :OFF_Q]
    q = _rope(z[..., OFF_Q:OFF_K].reshape(B, T, N_HEADS, 2, HEAD_DIM), pos)
    k = _rope(z[..., OFF_K:OFF_V].reshape(B, T, N_HEADS, 2, HEAD_DIM), pos)
    v = z[..., OFF_V:OFF_G].reshape(B, T, N_HEADS, 2 * HEAD_DIM)
    gate_pool = jax.nn.sigmoid(z[..., OFF_G:OFF_G + D_MODEL])
    gate_attn = jax.nn.sigmoid(z[..., OFF_G + D_MODEL:])

    u_full = u if pool_prev is None else jnp.concatenate([pool_prev.astype(u.dtype), u], axis=1)
    pool_out = _pool_mixer(u_full, pos, p['w_pool_grp'], p['pool_scale'])
    new_pool = u_full[:, -POOL_BUF:]

    lam_init = _lambda_init(layer)
    lam = (jnp.exp(jnp.sum(p['lambda_q1'].astype(jnp.float32) * p['lambda_k1'].astype(jnp.float32)))
           - jnp.exp(jnp.sum(p['lambda_q2'].astype(jnp.float32) * p['lambda_k2'].astype(jnp.float32)))
           + lam_init)
    if cache_k is None:
        o = _attend_prompt(q, k, v, lam)
    else:
        o = _attend_sample(q, k, v, cache_k, cache_v, page_table, layer, lam)
    o = _rms_norm(o, p['g_subln'], SUBLN_EPS) * (1.0 - lam_init)

    merged = (gate_pool * (pool_out @ p['w_pool_proj'])
              + gate_attn * (o.reshape(B, T, D_ATTN) @ p['w_attn_proj']))
    x = x + merged @ p['w_o']
    h2 = _rms_norm(x, p['g_mlp'], RMS_EPS)
    x = x + jnp.square(jax.nn.relu(h2 @ p['w_up'])) @ p['w_down']
    k_rows = k.reshape(B, T, N_HEADS, 2 * HEAD_DIM)
    return x, k_rows, v, new_pool


def setup_inputs(seed: int = 0) -> dict:
    key = jax.random.key(seed)
    ks = jax.random.split(key, 24)
    f32 = jnp.float32

    def nrm(k, shape, scale):
        return jax.random.normal(k, shape, f32) * scale

    n_pages = PAST_LEN // PAGE_SIZE
    n_pool_pages = (DEC_BATCH * n_pages * 5) // 4
    page_table = jax.random.permutation(ks[5], n_pool_pages)[:DEC_BATCH * n_pages]
    page_table = page_table.reshape(DEC_BATCH, n_pages).astype(jnp.int32)
    return {
        'x_prompt': nrm(ks[0], (BATCH, SEQ, D_MODEL), 1.0),
        'x_sample': nrm(ks[1], (DEC_BATCH, DEC_SEQ, D_MODEL), 1.0),
        'state_pool': nrm(ks[2], (DEPTH, DEC_BATCH, POOL_BUF, D_POOL), 1.0),
        'cache_k': nrm(ks[3], (DEPTH, n_pool_pages, PAGE_SIZE, N_HEADS, 2 * HEAD_DIM), 1.0),
        'cache_v': nrm(ks[4], (DEPTH, n_pool_pages, PAGE_SIZE, N_HEADS, 2 * HEAD_DIM), 1.0),
        'page_table': page_table,
        'g_mix': 1.0 + nrm(ks[6], (DEPTH, D_MODEL), 0.05),
        'w_in': nrm(ks[7], (DEPTH, D_MODEL, D_IN), D_MODEL ** -0.5),
        'w_pool_grp': nrm(ks[8], (DEPTH, N_POOL_GROUPS, POOL_GROUP, POOL_GROUP), POOL_GROUP ** -0.5),
        'pool_scale': 1.0 + nrm(ks[9], (DEPTH, D_POOL), 0.1),
        'w_pool_proj': nrm(ks[10], (DEPTH, D_POOL, D_MODEL), D_POOL ** -0.5),
        'lambda_q1': nrm(ks[11], (DEPTH, HEAD_DIM), 0.1),
        'lambda_k1': nrm(ks[12], (DEPTH, HEAD_DIM), 0.1),
        'lambda_q2': nrm(ks[13], (DEPTH, HEAD_DIM), 0.1),
        'lambda_k2': nrm(ks[14], (DEPTH, HEAD_DIM), 0.1),
        'g_subln': 1.0 + nrm(ks[15], (DEPTH, 2 * HEAD_DIM), 0.05),
        'w_attn_proj': nrm(ks[16], (DEPTH, D_ATTN, D_MODEL), D_ATTN ** -0.5),
        'w_o': nrm(ks[17], (DEPTH, D_MODEL, D_MODEL), D_MODEL ** -0.5),
        'g_mlp': 1.0 + nrm(ks[18], (DEPTH, D_MODEL), 0.05),
        'w_up': nrm(ks[19], (DEPTH, D_MODEL, D_FF), D_MODEL ** -0.5),
        'w_down': nrm(ks[20], (DEPTH, D_FF, D_MODEL), D_FF ** -0.5),
        'g_final': 1.0 + nrm(ks[21], (D_MODEL,), 0.05),
    }


def reference(x_prompt, x_sample, state_pool, cache_k, cache_v, page_table,
              g_mix, w_in, w_pool_grp, pool_scale, w_pool_proj,
              lambda_q1, lambda_k1, lambda_q2, lambda_k2, g_subln,
              w_attn_proj, w_o, g_mlp, w_up, w_down, g_final):
    T_p = x_prompt.shape[1]
    T_s = x_sample.shape[1]
    past = page_table.shape[1] * PAGE_SIZE
    pos_prompt = jnp.arange(T_p, dtype=jnp.int32)
    pos_sample = past + jnp.arange(T_s, dtype=jnp.int32)
    xp, xs = x_prompt, x_sample
    kp_l, vp_l, pp_l, ks_l, vs_l, ps_l = [], [], [], [], [], []
    for layer in range(DEPTH):
        p = {
            'g_mix': g_mix[layer], 'w_in': w_in[layer], 'w_pool_grp': w_pool_grp[layer],
            'pool_scale': pool_scale[layer], 'w_pool_proj': w_pool_proj[layer],
            'lambda_q1': lambda_q1[layer], 'lambda_k1': lambda_k1[layer],
            'lambda_q2': lambda_q2[layer], 'lambda_k2': lambda_k2[layer],
            'g_subln': g_subln[layer], 'w_attn_proj': w_attn_proj[layer], 'w_o': w_o[layer],
            'g_mlp': g_mlp[layer], 'w_up': w_up[layer], 'w_down': w_down[layer],
        }
        xp, kp, vp, pp = _decoder_layer(xp, pos_prompt, None, None, None, None, layer, p)
        xs, kn, vn, pn = _decoder_layer(xs, pos_sample, state_pool[layer], cache_k, cache_v,
                                        page_table, layer, p)
        kp_l.append(kp); vp_l.append(vp); pp_l.append(pp)
        ks_l.append(kn); vs_l.append(vn); ps_l.append(pn)
    y_prompt = _rms_norm(xp, g_final, RMS_EPS)
    y_sample = _rms_norm(xs, g_final, RMS_EPS)
    new_k_prompt = jnp.stack(kp_l, axis=0)
    new_v_prompt = jnp.stack(vp_l, axis=0)
    new_pool_prompt = jnp.stack(pp_l, axis=0)
    new_k_sample = jnp.stack(ks_l, axis=0)
    new_v_sample = jnp.stack(vs_l, axis=0)
    new_pool_sample = jnp.stack(ps_l, axis=0)
    return (y_prompt, y_sample, new_k_prompt, new_v_prompt, new_pool_prompt,
            new_k_sample, new_v_sample, new_pool_sample)
```

```python
import functools
import math

import jax
import jax.numpy as jnp
from jax import lax
from jax.experimental import pallas as pl
from jax.experimental.pallas import tpu as pltpu

D_MODEL = 1024
N_HEADS = 8
HEAD_DIM = 64
HEAD_W = 2 * HEAD_DIM
D_ATTN = N_HEADS * HEAD_W
POOL_WINDOWS = (2, 4, 8, 16)
D_POOL = D_MODEL // 2
POOL_GROUP = D_POOL // len(POOL_WINDOWS)
POOL_BUF = max(POOL_WINDOWS) - 1
HALO = 16
D_FF = 4 * D_MODEL
ROPE_THETA = 10000.0
RMS_EPS = 1e-6
SUBLN_EPS = 1e-5
NEG_INF = -1e30
PAGE_SIZE = 128
OFF_Q = D_POOL
OFF_K = OFF_Q + D_ATTN
OFF_V = OFF_K + D_ATTN
OFF_G = OFF_V + D_ATTN
D_IN = OFF_G + 2 * D_MODEL

LANES = 128
SUBLANES = 8
VMEM_LIMIT = 56 * 1024 * 1024

BF16 = jnp.bfloat16
F32 = jnp.float32


def _lambda_init(layer):
    return 0.8 - 0.6 * math.exp(-0.3 * layer)


def _resident(shape):
    nd = len(shape)
    return pl.BlockSpec(shape, lambda *_: (0,) * nd, pipeline_mode=pl.Buffered(1))


def _params(*sem):
    return pltpu.CompilerParams(dimension_semantics=sem, vmem_limit_bytes=VMEM_LIMIT)


def _prep_kernel(invf_ref, lq1_ref, lk1_ref, lq2_ref, lk2_ref, cos_ref, sin_ref, lam_ref, *, n_prompt, past_len, lam_init):
    rows = cos_ref.shape[0]
    row = lax.broadcasted_iota(jnp.int32, (rows, LANES), 0)
    lane = lax.broadcasted_iota(jnp.int32, (rows, LANES), 1)
    pos = jnp.where(row < n_prompt, row, past_len).astype(F32)
    ang = pos * invf_ref[...]
    cos_ref[...] = jnp.cos(ang)
    sin_ref[...] = jnp.where((lane % HEAD_DIM) < HEAD_DIM // 2, -jnp.sin(ang), jnp.sin(ang))
    d1 = jnp.sum(lq1_ref[...] * lk1_ref[...], axis=-1, keepdims=True)
    d2 = jnp.sum(lq2_ref[...] * lk2_ref[...], axis=-1, keepdims=True)
    lam = jnp.exp(d1) - jnp.exp(d2) + lam_init
    lam_ref[...] = jnp.broadcast_to(lam, lam_ref.shape)


def _prep(lq1, lk1, lq2, lk2, *, n_prompt, n_sample_rows, past_len, lam_init):
    half = HEAD_DIM // 2
    inv_freq = ROPE_THETA ** (-jnp.arange(half, dtype=F32) * (2.0 / HEAD_DIM))
    invf = jnp.tile(inv_freq, LANES // half)[None, :]
    rows = n_prompt + n_sample_rows
    kern = functools.partial(_prep_kernel, n_prompt=n_prompt, past_len=past_len, lam_init=lam_init)
    return pl.pallas_call(
        kern,
        out_shape=(jax.ShapeDtypeStruct((rows, LANES), F32),
                   jax.ShapeDtypeStruct((rows, LANES), F32),
                   jax.ShapeDtypeStruct((SUBLANES, LANES), F32)),
        name="prep",
    )(invf, lq1[None, :], lk1[None, :], lq2[None, :], lk2[None, :])


PROJ_CHUNK = 512


def _rope(z, cos, sin, first_half):
    swapped = jnp.where(first_half, pltpu.roll(z, LANES - HEAD_DIM // 2, 1), pltpu.roll(z, HEAD_DIM // 2, 1))
    return z * cos + swapped * sin


def _proj_kernel(x_ref, g_ref, w_ref, cos_ref, sin_ref,
                 u_ref, q_ref, k_ref, v_ref, kb_ref, vb_ref, gate_ref, h_scr):
    x = x_ref[...]
    ms = jnp.mean(x * x, axis=-1, keepdims=True)
    h_scr[...] = (x * lax.rsqrt(ms + RMS_EPS) * g_ref[...]).astype(BF16)
    tm = x.shape[0]
    cos = cos_ref[...]
    sin = sin_ref[...]
    lane = lax.broadcasted_iota(jnp.int32, (tm, LANES), 1)
    first_half = (lane % HEAD_DIM) < HEAD_DIM // 2

    def dot(off, width):
        return jnp.dot(h_scr[...], w_ref[:, off:off + width], preferred_element_type=F32)

    u_ref[...] = dot(0, D_POOL)
    for c in range(D_ATTN // PROJ_CHUNK):
        zq = dot(OFF_Q + c * PROJ_CHUNK, PROJ_CHUNK)
        zk = dot(OFF_K + c * PROJ_CHUNK, PROJ_CHUNK)
        for j in range(PROJ_CHUNK // LANES):
            sl = slice(j * LANES, (j + 1) * LANES)
            osl = slice(c * PROJ_CHUNK + j * LANES, c * PROJ_CHUNK + (j + 1) * LANES)
            q_ref[:, osl] = (_rope(zq[:, sl], cos, sin, first_half) * (HEAD_DIM ** -0.5)).astype(BF16)
            kr = _rope(zk[:, sl], cos, sin, first_half)
            k_ref[:, osl] = kr
            kb_ref[:, osl] = kr.astype(BF16)
        zv = dot(OFF_V + c * PROJ_CHUNK, PROJ_CHUNK)
        csl = slice(c * PROJ_CHUNK, (c + 1) * PROJ_CHUNK)
        v_ref[:, csl] = zv
        vb_ref[:, csl] = zv.astype(BF16)
    for c in range(2 * D_MODEL // PROJ_CHUNK):
        zg = dot(OFF_G + c * PROJ_CHUNK, PROJ_CHUNK)
        gate_ref[:, c * PROJ_CHUNK:(c + 1) * PROJ_CHUNK] = jax.nn.sigmoid(zg)


def _proj(x2d, g_mix, w_in_b, cos_tab, sin_tab, *, tm, tab_block_of):
    m = x2d.shape[0]
    row = lambda i: (i, 0)
    tab = pl.BlockSpec((tm, LANES), lambda i: (tab_block_of(i), 0))
    outs = (
        jax.ShapeDtypeStruct((m, D_POOL), F32),
        jax.ShapeDtypeStruct((m, D_ATTN), BF16),
        jax.ShapeDtypeStruct((m, D_ATTN), F32),
        jax.ShapeDtypeStruct((m, D_ATTN), F32),
        jax.ShapeDtypeStruct((m, D_ATTN), BF16),
        jax.ShapeDtypeStruct((m, D_ATTN), BF16),
        jax.ShapeDtypeStruct((m, 2 * D_MODEL), F32),
    )
    return pl.pallas_call(
        _proj_kernel,
        grid=(m // tm,),
        in_specs=[pl.BlockSpec((tm, D_MODEL), row), _resident((1, D_MODEL)), _resident((D_MODEL, D_IN)), tab, tab],
        out_specs=tuple(pl.BlockSpec((tm, s.shape[1]), row) for s in outs),
        out_shape=outs,
        scratch_shapes=[pltpu.VMEM((tm, D_MODEL), BF16)],
        compiler_params=_params("parallel"),
        name="proj",
    )(x2d, g_mix, w_in_b, cos_tab, sin_tab)


def _pool_prompt_kernel(u_ref, halo_ref, o_ref, ext_scr, *, tm):
    i = pl.program_id(1)
    ext_scr[0:HALO, :] = jnp.where(i > 0, halo_ref[...], 0.0)
    ext_scr[HALO:HALO + tm, :] = u_ref[...]
    pos = i * tm + lax.broadcasted_iota(jnp.int32, (tm, POOL_GROUP), 0)
    for g, w in enumerate(POOL_WINDOWS):
        sl = slice(g * POOL_GROUP, (g + 1) * POOL_GROUP)
        cur = ext_scr[HALO:HALO + tm, sl]
        acc = cur
        for k in range(1, w):
            acc = acc + ext_scr[HALO - k:HALO - k + tm, sl]
        count = jnp.minimum(pos + 1, w).astype(F32)
        o_ref[:, sl] = (acc / count - cur).astype(o_ref.dtype)


def _pool_prompt(u3d, *, tm):
    b, t, _ = u3d.shape
    kern = functools.partial(_pool_prompt_kernel, tm=tm)
    return pl.pallas_call(
        kern,
        grid=(b, t // tm),
        in_specs=[pl.BlockSpec((None, tm, D_POOL), lambda bi, i: (bi, i, 0)),
                  pl.BlockSpec((None, HALO, D_POOL), lambda bi, i: (bi, jnp.maximum(i * (tm // HALO) - 1, 0), 0))],
        out_specs=pl.BlockSpec((None, tm, D_POOL), lambda bi, i: (bi, i, 0)),
        out_shape=jax.ShapeDtypeStruct((b, t, D_POOL), BF16),
        scratch_shapes=[pltpu.VMEM((HALO + tm, D_POOL), F32)],
        compiler_params=_params("parallel", "parallel"),
        name="pool_prompt",
    )(u3d, u3d)


def _pool_sample_kernel(state_ref, u_ref, o_ref, *, past_len):
    for g, w in enumerate(POOL_WINDOWS):
        sl = slice(g * POOL_GROUP, (g + 1) * POOL_GROUP)
        cur = u_ref[:, sl]
        acc = cur
        for k in range(1, w):
            acc = acc + state_ref[POOL_BUF - k, :, sl]
        count = float(min(past_len + 1, w))
        o_ref[:, sl] = (acc / count - cur).astype(o_ref.dtype)


def _pool_sample(state_t, u_s, *, past_len):
    n = u_s.shape[0]
    return pl.pallas_call(
        functools.partial(_pool_sample_kernel, past_len=past_len),
        out_shape=jax.ShapeDtypeStruct((n, D_POOL), BF16),
        name="pool_sample",
    )(state_t, u_s)


def _subln(o, g, lam_init):
    ms = jnp.mean(o * o, axis=-1, keepdims=True)
    return o * lax.rsqrt(ms + SUBLN_EPS) * g * (1.0 - lam_init)


def _attn_prompt_kernel(lam_ref, g_ref, q_ref, k_ref, v_ref, o_ref,
                        m_scr, l_scr, acc_scr, *, tq, tk, lam_init):
    qi = pl.program_id(2)
    q = q_ref[...].astype(F32)
    lane = lax.broadcasted_iota(jnp.int32, (tq, HEAD_W), 1)
    qmaps = (jnp.where(lane < HEAD_DIM, q, 0.0).astype(BF16), jnp.where(lane >= HEAD_DIM, q, 0.0).astype(BF16))
    m_scr[...] = jnp.full(m_scr.shape, NEG_INF, F32)
    l_scr[...] = jnp.zeros(l_scr.shape, F32)
    acc_scr[...] = jnp.zeros(acc_scr.shape, F32)

    def block(start, mask):
        k = k_ref[pl.ds(start, tk), :]
        v = v_ref[pl.ds(start, tk), :]
        for c in range(2):
            s = lax.dot_general(qmaps[c], k, (((1,), (1,)), ((), ())), preferred_element_type=F32)
            if mask is not None:
                s = jnp.where(mask, s, NEG_INF)
            m_prev = m_scr[c]
            m_new = jnp.maximum(m_prev, jnp.max(s, axis=-1, keepdims=True))
            alpha = jnp.exp(m_prev - m_new)
            p = jnp.exp(s - m_new[:, 0:1])
            l_scr[c] = alpha * l_scr[c] + jnp.sum(p, axis=-1, keepdims=True)
            acc_scr[c] = alpha * acc_scr[c] + jnp.dot(p.astype(BF16), v, preferred_element_type=F32)
            m_scr[c] = m_new

    def body(j, carry):
        block(pl.multiple_of(j * tk, tk), None)
        return carry

    n_full = qi * (tq // tk)
    lax.fori_loop(0, n_full, body, 0)
    row = lax.broadcasted_iota(jnp.int32, (tq, tk), 0)
    col = lax.broadcasted_iota(jnp.int32, (tq, tk), 1)
    for d in range(tq // tk):
        block(pl.multiple_of(qi * tq + d * tk, tk), col + d * tk <= row)

    lam = lam_ref[0:1, :]
    o = acc_scr[0] / l_scr[0] - lam * (acc_scr[1] / l_scr[1])
    o_ref[...] = _subln(o, g_ref[...], lam_init).astype(o_ref.dtype)


def _attn_prompt(lam, g_subln, q, kb, vb, *, tq, tk, lam_init):
    b, t, _ = q.shape
    kern = functools.partial(_attn_prompt_kernel, tq=tq, tk=tk, lam_init=lam_init)
    return pl.pallas_call(
        kern,
        grid=(b, N_HEADS, t // tq),
        in_specs=[_resident((SUBLANES, LANES)), _resident((1, HEAD_W)),
                  pl.BlockSpec((None, tq, HEAD_W), lambda bi, h, i: (bi, i, h)),
                  pl.BlockSpec((None, t, HEAD_W), lambda bi, h, i: (bi, 0, h)),
                  pl.BlockSpec((None, t, HEAD_W), lambda bi, h, i: (bi, 0, h))],
        out_specs=pl.BlockSpec((None, tq, HEAD_W), lambda bi, h, i: (bi, i, h)),
        out_shape=jax.ShapeDtypeStruct((b, t, D_ATTN), BF16),
        scratch_shapes=[pltpu.VMEM((2, tq, HEAD_W), F32), pltpu.VMEM((2, tq, HEAD_W), F32),
                        pltpu.VMEM((2, tq, HEAD_W), F32)],
        compiler_params=_params("parallel", "parallel", "arbitrary"),
        name="attn_prompt",
    )(lam, g_subln, q, kb, vb)


N_MAPROWS = 2 * N_HEADS
KEYS_PER_PAGE = PAGE_SIZE * N_HEADS


def _attn_sample_kernel(pt_ref, lam_ref, g_ref, q_ref, kn_ref, vn_ref, *refs, n_pages, lam_init):
    del pt_ref
    k_refs = refs[:n_pages]
    v_refs = refs[n_pages:2 * n_pages]
    o_ref = refs[2 * n_pages]

    row = lax.broadcasted_iota(jnp.int32, (N_MAPROWS, HEAD_W), 0)
    lane = lax.broadcasted_iota(jnp.int32, (N_MAPROWS, HEAD_W), 1)
    q = q_ref[...].astype(F32)
    qm = jnp.where((lane // HEAD_DIM) == (row // N_HEADS), jnp.concatenate([q, q], axis=0), 0.0).astype(BF16)

    krow = lax.broadcasted_iota(jnp.int32, (N_MAPROWS, KEYS_PER_PAGE), 0)
    kcol = lax.broadcasted_iota(jnp.int32, (N_MAPROWS, KEYS_PER_PAGE), 1)
    own = (kcol % N_HEADS) == (krow % N_HEADS)

    def scores(keys):
        return lax.dot_general(qm, keys, (((1,), (1,)), ((), ())), preferred_element_type=F32)

    s_pages = [jnp.where(own, scores(kr[...].reshape(KEYS_PER_PAGE, HEAD_W).astype(BF16)), NEG_INF)
               for kr in k_refs]
    pad = jnp.zeros((LANES - N_HEADS, HEAD_W), F32)
    kn = jnp.concatenate([kn_ref[...], pad], axis=0).astype(BF16)
    vn = jnp.concatenate([vn_ref[...], pad], axis=0).astype(BF16)
    s_new = jnp.where(lane == (row % N_HEADS), scores(kn), NEG_INF)

    m = jnp.max(s_new, axis=-1, keepdims=True)
    for s in s_pages:
        m = jnp.maximum(m, jnp.max(s, axis=-1, keepdims=True))
    p_new = jnp.exp(s_new - m)
    l = jnp.sum(p_new, axis=-1, keepdims=True)
    acc = jnp.dot(p_new.astype(BF16), vn, preferred_element_type=F32)
    for s, vr in zip(s_pages, v_refs):
        p = jnp.exp(s - m)
        l = l + jnp.sum(p, axis=-1, keepdims=True)
        acc = acc + jnp.dot(p.astype(BF16), vr[...].reshape(KEYS_PER_PAGE, HEAD_W).astype(BF16),
                            preferred_element_type=F32)
    a = acc / l
    o = a[0:N_HEADS] - lam_ref[0:1, 0:1] * a[N_HEADS:N_MAPROWS]
    o_ref[...] = _subln(o, g_ref[...], lam_init).astype(o_ref.dtype)


def _attn_sample(page_table, lam, g_subln, q3, kn3, vn3, cache_k4, cache_v4, *, lam_init):
    n, n_pages = page_table.shape
    pt_flat = page_table.reshape(-1)

    def page_spec(j):
        return pl.BlockSpec((None, PAGE_SIZE, N_HEADS, HEAD_W), lambda bi, pt: (pt[bi * n_pages + j], 0, 0, 0))

    seq = pl.BlockSpec((None, N_HEADS, HEAD_W), lambda bi, pt: (bi, 0, 0))
    const = lambda bi, pt: (0, 0)
    grid_spec = pltpu.PrefetchScalarGridSpec(
        num_scalar_prefetch=1,
        grid=(n,),
        in_specs=[pl.BlockSpec((SUBLANES, LANES), const), pl.BlockSpec((1, HEAD_W), const), seq, seq, seq]
                 + [page_spec(j) for j in range(n_pages)] * 2,
        out_specs=seq,
    )
    kern = functools.partial(_attn_sample_kernel, n_pages=n_pages, lam_init=lam_init)
    return pl.pallas_call(
        kern,
        grid_spec=grid_spec,
        out_shape=jax.ShapeDtypeStruct((n, N_HEADS, HEAD_W), F32),
        compiler_params=_params("parallel"),
        name="attn_sample",
    )(pt_flat, lam, g_subln, q3, kn3, vn3, *([cache_k4] * n_pages), *([cache_v4] * n_pages))


def _mix_kernel(x_ref, pooled_ref, on_ref, gate_ref, wg_ref, scale_ref, wpp_ref, wap_ref, wo_ref, o_ref):
    parts = []
    for g in range(len(POOL_WINDOWS)):
        sl = slice(g * POOL_GROUP, (g + 1) * POOL_GROUP)
        parts.append(jnp.dot(pooled_ref[:, sl], wg_ref[g], preferred_element_type=F32))
    pool_out = jnp.concatenate(parts, axis=1) * scale_ref[...]
    pool_proj = jnp.dot(pool_out.astype(BF16), wpp_ref[...], preferred_element_type=F32)
    attn_proj = jnp.dot(on_ref[...].astype(BF16), wap_ref[...], preferred_element_type=F32)
    merged = gate_ref[:, 0:D_MODEL] * pool_proj + gate_ref[:, D_MODEL:2 * D_MODEL] * attn_proj
    o_ref[...] = x_ref[...] + jnp.dot(merged.astype(BF16), wo_ref[...], preferred_element_type=F32)


def _mix(x2d, pooled, on, gates, wg_b, pool_scale, wpp_b, wap_b, wo_b, *, tm):
    m = x2d.shape[0]
    row = lambda i: (i, 0)
    return pl.pallas_call(
        _mix_kernel,
        grid=(m // tm,),
        in_specs=[pl.BlockSpec((tm, D_MODEL), row), pl.BlockSpec((tm, D_POOL), row),
                  pl.BlockSpec((tm, D_ATTN), row), pl.BlockSpec((tm, 2 * D_MODEL), row),
                  _resident(wg_b.shape), _resident((1, D_POOL)), _resident(wpp_b.shape),
                  _resident(wap_b.shape), _resident(wo_b.shape)],
        out_specs=pl.BlockSpec((tm, D_MODEL), row),
        out_shape=jax.ShapeDtypeStruct((m, D_MODEL), F32),
        compiler_params=_params("parallel"),
        name="mix",
    )(x2d, pooled, on, gates, wg_b, pool_scale, wpp_b, wap_b, wo_b)


MLP_CHUNK = 1024


def _rms(x, g):
    ms = jnp.mean(x * x, axis=-1, keepdims=True)
    return x * lax.rsqrt(ms + RMS_EPS) * g


def _mlp_kernel(x_ref, g_ref, wup_ref, wdn_ref, gf_ref, o_ref, h_scr):
    x = x_ref[...]
    h_scr[...] = _rms(x, g_ref[...]).astype(BF16)
    y = x
    for c in range(D_FF // MLP_CHUNK):
        sl = slice(c * MLP_CHUNK, (c + 1) * MLP_CHUNK)
        a = jnp.maximum(jnp.dot(h_scr[...], wup_ref[:, sl], preferred_element_type=F32), 0.0)
        y = y + jnp.dot((a * a).astype(BF16), wdn_ref[sl, :], preferred_element_type=F32)
    o_ref[...] = _rms(y, gf_ref[...])


def _mlp(x2d, g_mlp, wup_b, wdn_b, g_final, *, tm):
    m = x2d.shape[0]
    row = lambda i: (i, 0)
    return pl.pallas_call(
        _mlp_kernel,
        grid=(m // tm,),
        in_specs=[pl.BlockSpec((tm, D_MODEL), row), _resident((1, D_MODEL)), _resident(wup_b.shape),
                  _resident(wdn_b.shape), _resident((1, D_MODEL))],
        out_specs=pl.BlockSpec((tm, D_MODEL), row),
        out_shape=jax.ShapeDtypeStruct((m, D_MODEL), F32),
        scratch_shapes=[pltpu.VMEM((tm, D_MODEL), BF16)],
        compiler_params=_params("parallel"),
        name="mlp",
    )(x2d, g_mlp, wup_b, wdn_b, g_final)


PROMPT_TM = 512
ATTN_TQ = 512
ATTN_TK = 512


def kernel(x_prompt, x_sample, state_pool, cache_k, cache_v, page_table, g_mix, w_in, w_pool_grp, pool_scale,
           w_pool_proj, lambda_q1, lambda_k1, lambda_q2, lambda_k2, g_subln, w_attn_proj, w_o, g_mlp, w_up,
           w_down, g_final):
    depth = w_in.shape[0]
    assert depth == 1, "single-layer trunk"
    layer = 0
    lam_init = _lambda_init(layer)
    b, t, _ = x_prompt.shape
    n, t_s, _ = x_sample.shape
    assert t_s == 1 and t % PROMPT_TM == 0 and t % ATTN_TQ == 0 and ATTN_TQ % ATTN_TK == 0
    past_len = page_table.shape[1] * PAGE_SIZE

    w_in_b = w_in[layer].astype(BF16)
    wg_b = w_pool_grp[layer].astype(BF16)
    wpp_b = w_pool_proj[layer].astype(BF16)
    wap_b = w_attn_proj[layer].astype(BF16)
    wo_b = w_o[layer].astype(BF16)
    wup_b = w_up[layer].astype(BF16)
    wdn_b = w_down[layer].astype(BF16)
    g_mix2 = g_mix[layer][None, :]
    g_mlp2 = g_mlp[layer][None, :]
    g_fin2 = g_final[None, :]
    g_sub2 = g_subln[layer][None, :]
    scale2 = pool_scale[layer][None, :]

    cos_tab, sin_tab, lam = _prep(lambda_q1[layer], lambda_k1[layer], lambda_q2[layer], lambda_k2[layer],
                                  n_prompt=t, n_sample_rows=n, past_len=past_len, lam_init=lam_init)

    tiles_per_seq = t // PROMPT_TM
    xp2 = x_prompt.reshape(b * t, D_MODEL)
    u_p, q_p, k_p, v_p, kb_p, vb_p, gates_p = _proj(
        xp2, g_mix2, w_in_b, cos_tab, sin_tab, tm=PROMPT_TM, tab_block_of=lambda i: i % tiles_per_seq)
    u_p3 = u_p.reshape(b, t, D_POOL)
    pooled_p = _pool_prompt(u_p3, tm=PROMPT_TM).reshape(b * t, D_POOL)
    on_p = _attn_prompt(lam, g_sub2, q_p.reshape(b, t, D_ATTN), kb_p.reshape(b, t, D_ATTN),
                        vb_p.reshape(b, t, D_ATTN), tq=ATTN_TQ, tk=ATTN_TK, lam_init=lam_init)
    x1_p = _mix(xp2, pooled_p, on_p.reshape(b * t, D_ATTN), gates_p, wg_b, scale2, wpp_b, wap_b, wo_b,
                tm=PROMPT_TM)
    y_p = _mlp(x1_p, g_mlp2, wup_b, wdn_b, g_fin2, tm=PROMPT_TM)

    xs2 = x_sample.reshape(n, D_MODEL)
    u_s, q_s, k_s, v_s, _, _, gates_s = _proj(
        xs2, g_mix2, w_in_b, cos_tab, sin_tab, tm=n, tab_block_of=lambda i: t // n)
    state_t = jnp.swapaxes(state_pool[layer], 0, 1)
    pooled_s = _pool_sample(state_t, u_s, past_len=past_len)
    heads = lambda a: a.reshape(n, N_HEADS, HEAD_W)
    on_s = _attn_sample(page_table, lam, g_sub2, heads(q_s), heads(k_s), heads(v_s), cache_k[layer], cache_v[layer],
                        lam_init=lam_init)
    x1_s = _mix(xs2, pooled_s, on_s.reshape(n, D_ATTN), gates_s, wg_b, scale2, wpp_b, wap_b, wo_b, tm=n)
    y_s = _mlp(x1_s, g_mlp2, wup_b, wdn_b, g_fin2, tm=n)

    y_prompt = y_p.reshape(b, t, D_MODEL)
    y_sample = y_s.reshape(n, 1, D_MODEL)
    new_k_prompt = k_p.reshape(1, b, t, N_HEADS, HEAD_W)
    new_v_prompt = v_p.reshape(1, b, t, N_HEADS, HEAD_W)
    new_pool_prompt = u_p3[:, t - POOL_BUF:, :][None]
    new_k_sample = k_s.reshape(1, n, 1, N_HEADS, HEAD_W)
    new_v_sample = v_s.reshape(1, n, 1, N_HEADS, HEAD_W)
    new_pool_sample = jnp.concatenate([state_pool[layer][:, 1:, :], u_s[:, None, :]], axis=1)[None]
    return (y_prompt, y_sample, new_k_prompt, new_v_prompt, new_pool_prompt,
            new_k_sample, new_v_sample, new_pool_sample)
```

```python
import functools
import math

import jax
import jax.numpy as jnp
from jax import lax
from jax.experimental import pallas as pl
from jax.experimental.pallas import tpu as pltpu

D_MODEL = 1024
N_HEADS = 8
HEAD_DIM = 64
HEAD_W = 2 * HEAD_DIM
D_ATTN = N_HEADS * HEAD_W
POOL_WINDOWS = (2, 4, 8, 16)
D_POOL = D_MODEL // 2
POOL_GROUP = D_POOL // len(POOL_WINDOWS)
POOL_BUF = max(POOL_WINDOWS) - 1
HALO = 16
D_FF = 4 * D_MODEL
ROPE_THETA = 10000.0
RMS_EPS = 1e-6
SUBLN_EPS = 1e-5
NEG_INF = -1e30
PAGE_SIZE = 128
OFF_Q = D_POOL
OFF_K = OFF_Q + D_ATTN
OFF_V = OFF_K + D_ATTN
OFF_G = OFF_V + D_ATTN
D_IN = OFF_G + 2 * D_MODEL

LANES = 128
SUBLANES = 8
VMEM_LIMIT = 56 * 1024 * 1024

BF16 = jnp.bfloat16
F32 = jnp.float32


def _lambda_init(layer):
    return 0.8 - 0.6 * math.exp(-0.3 * layer)


def _resident(shape):
    nd = len(shape)
    return pl.BlockSpec(shape, lambda *_: (0,) * nd, pipeline_mode=pl.Buffered(1))


def _params(*sem):
    return pltpu.CompilerParams(dimension_semantics=sem, vmem_limit_bytes=VMEM_LIMIT)


def _prep_kernel(invf_ref, lq1_ref, lk1_ref, lq2_ref, lk2_ref, cos_ref, sin_ref, lam_ref, *, n_prompt, past_len, lam_init):
    rows = cos_ref.shape[0]
    row = lax.broadcasted_iota(jnp.int32, (rows, LANES), 0)
    lane = lax.broadcasted_iota(jnp.int32, (rows, LANES), 1)
    pos = jnp.where(row < n_prompt, row, past_len).astype(F32)
    ang = pos * invf_ref[...]
    cos_ref[...] = jnp.cos(ang)
    sin_ref[...] = jnp.where((lane % HEAD_DIM) < HEAD_DIM // 2, -jnp.sin(ang), jnp.sin(ang))
    d1 = jnp.sum(lq1_ref[...] * lk1_ref[...], axis=-1, keepdims=True)
    d2 = jnp.sum(lq2_ref[...] * lk2_ref[...], axis=-1, keepdims=True)
    lam = jnp.exp(d1) - jnp.exp(d2) + lam_init
    lam_ref[...] = jnp.broadcast_to(lam, lam_ref.shape)


def _prep(lq1, lk1, lq2, lk2, *, n_prompt, n_sample_rows, past_len, lam_init):
    half = HEAD_DIM // 2
    inv_freq = ROPE_THETA ** (-jnp.arange(half, dtype=F32) * (2.0 / HEAD_DIM))
    invf = jnp.tile(inv_freq, LANES // half)[None, :]
    rows = n_prompt + n_sample_rows
    kern = functools.partial(_prep_kernel, n_prompt=n_prompt, past_len=past_len, lam_init=lam_init)
    return pl.pallas_call(
        kern,
        out_shape=(jax.ShapeDtypeStruct((rows, LANES), F32),
                   jax.ShapeDtypeStruct((rows, LANES), F32),
                   jax.ShapeDtypeStruct((SUBLANES, LANES), F32)),
        name="prep",
    )(invf, lq1[None, :], lk1[None, :], lq2[None, :], lk2[None, :])


PROJ_CHUNK = 512
Q_SCALE = HEAD_DIM ** -0.5 * math.log2(math.e)


def _rope(z, cos, sin, first_half):
    swapped = jnp.where(first_half, pltpu.roll(z, LANES - HEAD_DIM // 2, 1), pltpu.roll(z, HEAD_DIM // 2, 1))
    return z * cos + swapped * sin


def _proj_kernel(x_ref, g_ref, w_ref, cos_ref, sin_ref,
                 u_ref, q_ref, k_ref, v_ref, kb_ref, vb_ref, gate_ref, h_scr):
    x = x_ref[...]
    ms = jnp.mean(x * x, axis=-1, keepdims=True)
    h_scr[...] = (x * lax.rsqrt(ms + RMS_EPS) * g_ref[...]).astype(BF16)
    tm = x.shape[0]
    cos = cos_ref[...]
    sin = sin_ref[...]
    lane = lax.broadcasted_iota(jnp.int32, (tm, LANES), 1)
    first_half = (lane % HEAD_DIM) < HEAD_DIM // 2

    def dot(off, width):
        return jnp.dot(h_scr[...], w_ref[:, off:off + width], preferred_element_type=F32)

    u_ref[...] = dot(0, D_POOL)
    for c in range(D_ATTN // PROJ_CHUNK):
        zq = dot(OFF_Q + c * PROJ_CHUNK, PROJ_CHUNK)
        zk = dot(OFF_K + c * PROJ_CHUNK, PROJ_CHUNK)
        for j in range(PROJ_CHUNK // LANES):
            sl = slice(j * LANES, (j + 1) * LANES)
            osl = slice(c * PROJ_CHUNK + j * LANES, c * PROJ_CHUNK + (j + 1) * LANES)
            q_ref[:, osl] = (_rope(zq[:, sl], cos, sin, first_half) * Q_SCALE).astype(BF16)
            kr = _rope(zk[:, sl], cos, sin, first_half)
            k_ref[:, osl] = kr
            kb_ref[:, osl] = kr.astype(BF16)
        zv = dot(OFF_V + c * PROJ_CHUNK, PROJ_CHUNK)
        csl = slice(c * PROJ_CHUNK, (c + 1) * PROJ_CHUNK)
        v_ref[:, csl] = zv
        vb_ref[:, csl] = zv.astype(BF16)
    for c in range(2 * D_MODEL // PROJ_CHUNK):
        zg = dot(OFF_G + c * PROJ_CHUNK, PROJ_CHUNK)
        gate_ref[:, c * PROJ_CHUNK:(c + 1) * PROJ_CHUNK] = jax.nn.sigmoid(zg)


def _proj(x2d, g_mix, w_in_b, cos_tab, sin_tab, *, tm, tab_block_of):
    m = x2d.shape[0]
    row = lambda i: (i, 0)
    tab = pl.BlockSpec((tm, LANES), lambda i: (tab_block_of(i), 0))
    outs = (
        jax.ShapeDtypeStruct((m, D_POOL), F32),
        jax.ShapeDtypeStruct((m, D_ATTN), BF16),
        jax.ShapeDtypeStruct((m, D_ATTN), F32),
        jax.ShapeDtypeStruct((m, D_ATTN), F32),
        jax.ShapeDtypeStruct((m, D_ATTN), BF16),
        jax.ShapeDtypeStruct((m, D_ATTN), BF16),
        jax.ShapeDtypeStruct((m, 2 * D_MODEL), F32),
    )
    return pl.pallas_call(
        _proj_kernel,
        grid=(m // tm,),
        in_specs=[pl.BlockSpec((tm, D_MODEL), row), _resident((1, D_MODEL)), _resident((D_MODEL, D_IN)), tab, tab],
        out_specs=tuple(pl.BlockSpec((tm, s.shape[1]), row) for s in outs),
        out_shape=outs,
        scratch_shapes=[pltpu.VMEM((tm, D_MODEL), BF16)],
        compiler_params=_params("parallel"),
        name="proj",
    )(x2d, g_mix, w_in_b, cos_tab, sin_tab)


def _pool_prompt_kernel(u_ref, halo_ref, o_ref, ext_scr, *, tm):
    i = pl.program_id(1)
    ext_scr[0:HALO, :] = jnp.where(i > 0, halo_ref[...], 0.0)
    ext_scr[HALO:HALO + tm, :] = u_ref[...]
    pos = i * tm + lax.broadcasted_iota(jnp.int32, (tm, POOL_GROUP), 0)
    for g, w in enumerate(POOL_WINDOWS):
        sl = slice(g * POOL_GROUP, (g + 1) * POOL_GROUP)
        cur = ext_scr[HALO:HALO + tm, sl]
        acc = cur
        for k in range(1, w):
            acc = acc + ext_scr[HALO - k:HALO - k + tm, sl]
        count = jnp.minimum(pos + 1, w).astype(F32)
        o_ref[:, sl] = (acc / count - cur).astype(o_ref.dtype)


def _pool_prompt(u3d, *, tm):
    b, t, _ = u3d.shape
    kern = functools.partial(_pool_prompt_kernel, tm=tm)
    return pl.pallas_call(
        kern,
        grid=(b, t // tm),
        in_specs=[pl.BlockSpec((None, tm, D_POOL), lambda bi, i: (bi, i, 0)),
                  pl.BlockSpec((None, HALO, D_POOL), lambda bi, i: (bi, jnp.maximum(i * (tm // HALO) - 1, 0), 0))],
        out_specs=pl.BlockSpec((None, tm, D_POOL), lambda bi, i: (bi, i, 0)),
        out_shape=jax.ShapeDtypeStruct((b, t, D_POOL), BF16),
        scratch_shapes=[pltpu.VMEM((HALO + tm, D_POOL), F32)],
        compiler_params=_params("parallel", "parallel"),
        name="pool_prompt",
    )(u3d, u3d)


def _pool_sample_kernel(state_ref, u_ref, o_ref, *, past_len):
    for g, w in enumerate(POOL_WINDOWS):
        sl = slice(g * POOL_GROUP, (g + 1) * POOL_GROUP)
        cur = u_ref[:, sl]
        acc = cur
        for k in range(1, w):
            acc = acc + state_ref[POOL_BUF - k, :, sl]
        count = float(min(past_len + 1, w))
        o_ref[:, sl] = (acc / count - cur).astype(o_ref.dtype)


def _pool_sample(state_t, u_s, *, past_len):
    n = u_s.shape[0]
    return pl.pallas_call(
        functools.partial(_pool_sample_kernel, past_len=past_len),
        out_shape=jax.ShapeDtypeStruct((n, D_POOL), BF16),
        name="pool_sample",
    )(state_t, u_s)


def _subln(o, g, lam_init):
    ms = jnp.mean(o * o, axis=-1, keepdims=True)
    return o * lax.rsqrt(ms + SUBLN_EPS) * g * (1.0 - lam_init)


ONES_ROWS = 16
VT_ROWS = HEAD_W + ONES_ROWS


def _attn_prompt_kernel(lam_ref, g_ref, q_ref, k_ref, v_ref, o_ref,
                        vt_scr, s0_scr, s1_scr, m_scr, acc_scr, *, blk, lam_init):
    qi = pl.program_id(2)
    nblk = k_ref.shape[0] // blk

    @pl.when(qi == 0)
    def _():
        for r in range(nblk):
            vt_scr[r, 0:HEAD_W, :] = v_ref[r * blk:(r + 1) * blk, :].astype(F32).T.astype(BF16)
            vt_scr[r, HEAD_W:VT_ROWS, :] = jnp.ones((ONES_ROWS, blk), BF16)

    qt = q_ref[...].astype(F32).T
    drow = lax.broadcasted_iota(jnp.int32, (HEAD_W, blk), 0)
    qmaps = (jnp.where(drow < HEAD_DIM, qt, 0.0).astype(BF16), jnp.where(drow >= HEAD_DIM, qt, 0.0).astype(BF16))
    m_scr[...] = jnp.full(m_scr.shape, NEG_INF, F32)
    acc_scr[...] = jnp.zeros(acc_scr.shape, F32)
    krow = lax.broadcasted_iota(jnp.int32, (blk, blk), 0)
    qcol = lax.broadcasted_iota(jnp.int32, (blk, blk), 1)
    causal = krow <= qcol

    def fill(s_scr, j):
        k = k_ref[pl.ds(pl.multiple_of(j * blk, blk), blk), :]
        for c in range(2):
            s_scr[c] = jnp.dot(k, qmaps[c], preferred_element_type=F32)

    def consume(s_scr, j, mask):
        vt = vt_scr[j]
        for c in range(2):
            s = s_scr[c]
            if mask is not None:
                s = jnp.where(mask, s, NEG_INF)
            m_prev = m_scr[c]
            m_new = jnp.maximum(m_prev, jnp.max(s, axis=0, keepdims=True))
            p = jnp.exp2(s - m_new).astype(BF16)
            acc_scr[c] = jnp.exp2(m_prev - m_new) * acc_scr[c] + jnp.dot(vt, p, preferred_element_type=F32)
            m_scr[c] = m_new

    def pair(i, carry):
        fill(s1_scr, 2 * i + 1)
        consume(s0_scr, 2 * i, None)
        fill(s0_scr, 2 * i + 2)
        consume(s1_scr, 2 * i + 1, None)
        return carry

    def run(odd):
        fill(s0_scr, 0)
        lax.fori_loop(0, qi // 2, pair, 0)
        if odd:
            fill(s1_scr, qi)
            consume(s0_scr, qi - 1, None)
            consume(s1_scr, qi, causal)
        else:
            consume(s0_scr, qi, causal)

    pl.when(qi % 2 == 0)(functools.partial(run, False))
    pl.when(qi % 2 == 1)(functools.partial(run, True))

    lam = lam_ref[0:1, 0:1]
    a1 = acc_scr[0]
    a2 = acc_scr[1]
    ot = (a1[0:HEAD_W] * (1.0 / a1[HEAD_W:HEAD_W + 1])
          - lam * (a2[0:HEAD_W] * (1.0 / a2[HEAD_W:HEAD_W + 1])))
    ms = jnp.mean(ot * ot, axis=0, keepdims=True)
    on = (ot * lax.rsqrt(ms + SUBLN_EPS)).T * (g_ref[...] * (1.0 - lam_init))
    o_ref[...] = on.astype(o_ref.dtype)


def _attn_prompt(lam, g_subln, q, kb, vb, *, blk, lam_init):
    b, t, _ = q.shape
    kern = functools.partial(_attn_prompt_kernel, blk=blk, lam_init=lam_init)
    return pl.pallas_call(
        kern,
        grid=(b, N_HEADS, t // blk),
        in_specs=[_resident((SUBLANES, LANES)), _resident((1, HEAD_W)),
                  pl.BlockSpec((None, blk, HEAD_W), lambda bi, h, i: (bi, i, h)),
                  pl.BlockSpec((None, t, HEAD_W), lambda bi, h, i: (bi, 0, h)),
                  pl.BlockSpec((None, t, HEAD_W), lambda bi, h, i: (bi, 0, h))],
        out_specs=pl.BlockSpec((None, blk, HEAD_W), lambda bi, h, i: (bi, i, h)),
        out_shape=jax.ShapeDtypeStruct((b, t, D_ATTN), BF16),
        scratch_shapes=[pltpu.VMEM((t // blk, VT_ROWS, blk), BF16),
                        pltpu.VMEM((2, blk, blk), F32), pltpu.VMEM((2, blk, blk), F32),
                        pltpu.VMEM((2, 1, blk), F32), pltpu.VMEM((2, VT_ROWS, blk), F32)],
        compiler_params=_params("parallel", "parallel", "arbitrary"),
        name="attn_prompt",
    )(lam, g_subln, q, kb, vb)


N_MAPROWS = 2 * N_HEADS
KEYS_PER_PAGE = PAGE_SIZE * N_HEADS


def _attn_sample_kernel(pt_ref, lam_ref, g_ref, q_ref, kn_ref, vn_ref, *refs, n_pages, lam_init):
    del pt_ref
    k_refs = refs[:n_pages]
    v_refs = refs[n_pages:2 * n_pages]
    o_ref = refs[2 * n_pages]

    row = lax.broadcasted_iota(jnp.int32, (N_MAPROWS, HEAD_W), 0)
    lane = lax.broadcasted_iota(jnp.int32, (N_MAPROWS, HEAD_W), 1)
    q = q_ref[...].astype(F32)
    qm = jnp.where((lane // HEAD_DIM) == (row // N_HEADS), jnp.concatenate([q, q], axis=0), 0.0).astype(BF16)

    krow = lax.broadcasted_iota(jnp.int32, (N_MAPROWS, KEYS_PER_PAGE), 0)
    kcol = lax.broadcasted_iota(jnp.int32, (N_MAPROWS, KEYS_PER_PAGE), 1)
    own = (kcol % N_HEADS) == (krow % N_HEADS)

    def scores(keys):
        return lax.dot_general(qm, keys, (((1,), (1,)), ((), ())), preferred_element_type=F32)

    s_pages = [jnp.where(own, scores(kr[...].reshape(KEYS_PER_PAGE, HEAD_W).astype(BF16)), NEG_INF)
               for kr in k_refs]
    pad = jnp.zeros((LANES - N_HEADS, HEAD_W), F32)
    kn = jnp.concatenate([kn_ref[...], pad], axis=0).astype(BF16)
    vn = jnp.concatenate([vn_ref[...], pad], axis=0).astype(BF16)
    s_new = jnp.where(lane == (row % N_HEADS), scores(kn), NEG_INF)

    m = jnp.max(s_new, axis=-1, keepdims=True)
    for s in s_pages:
        m = jnp.maximum(m, jnp.max(s, axis=-1, keepdims=True))
    p_new = jnp.exp2(s_new - m)
    l = jnp.sum(p_new, axis=-1, keepdims=True)
    acc = jnp.dot(p_new.astype(BF16), vn, preferred_element_type=F32)
    for s, vr in zip(s_pages, v_refs):
        p = jnp.exp2(s - m)
        l = l + jnp.sum(p, axis=-1, keepdims=True)
        acc = acc + jnp.dot(p.astype(BF16), vr[...].reshape(KEYS_PER_PAGE, HEAD_W).astype(BF16),
                            preferred_element_type=F32)
    a = acc / l
    o = a[0:N_HEADS] - lam_ref[0:1, 0:1] * a[N_HEADS:N_MAPROWS]
    o_ref[...] = _subln(o, g_ref[...], lam_init).astype(o_ref.dtype)


def _attn_sample(page_table, lam, g_subln, q3, kn3, vn3, cache_k4, cache_v4, *, lam_init):
    n, n_pages = page_table.shape
    pt_flat = page_table.reshape(-1)

    def page_spec(j):
        return pl.BlockSpec((None, PAGE_SIZE, N_HEADS, HEAD_W), lambda bi, pt: (pt[bi * n_pages + j], 0, 0, 0))

    seq = pl.BlockSpec((None, N_HEADS, HEAD_W), lambda bi, pt: (bi, 0, 0))
    const = lambda bi, pt: (0, 0)
    grid_spec = pltpu.PrefetchScalarGridSpec(
        num_scalar_prefetch=1,
        grid=(n,),
        in_specs=[pl.BlockSpec((SUBLANES, LANES), const), pl.BlockSpec((1, HEAD_W), const), seq, seq, seq]
                 + [page_spec(j) for j in range(n_pages)] * 2,
        out_specs=seq,
    )
    kern = functools.partial(_attn_sample_kernel, n_pages=n_pages, lam_init=lam_init)
    return pl.pallas_call(
        kern,
        grid_spec=grid_spec,
        out_shape=jax.ShapeDtypeStruct((n, N_HEADS, HEAD_W), F32),
        compiler_params=_params("parallel"),
        name="attn_sample",
    )(pt_flat, lam, g_subln, q3, kn3, vn3, *([cache_k4] * n_pages), *([cache_v4] * n_pages))


def _mix_kernel(x_ref, pooled_ref, on_ref, gate_ref, wg_ref, scale_ref, wpp_ref, wap_ref, wo_ref, o_ref):
    parts = []
    for g in range(len(POOL_WINDOWS)):
        sl = slice(g * POOL_GROUP, (g + 1) * POOL_GROUP)
        parts.append(jnp.dot(pooled_ref[:, sl], wg_ref[g], preferred_element_type=F32))
    pool_out = jnp.concatenate(parts, axis=1) * scale_ref[...]
    pool_proj = jnp.dot(pool_out.astype(BF16), wpp_ref[...], preferred_element_type=F32)
    attn_proj = jnp.dot(on_ref[...].astype(BF16), wap_ref[...], preferred_element_type=F32)
    merged = gate_ref[:, 0:D_MODEL] * pool_proj + gate_ref[:, D_MODEL:2 * D_MODEL] * attn_proj
    o_ref[...] = x_ref[...] + jnp.dot(merged.astype(BF16), wo_ref[...], preferred_element_type=F32)


def _mix(x2d, pooled, on, gates, wg_b, pool_scale, wpp_b, wap_b, wo_b, *, tm):
    m = x2d.shape[0]
    row = lambda i: (i, 0)
    return pl.pallas_call(
        _mix_kernel,
        grid=(m // tm,),
        in_specs=[pl.BlockSpec((tm, D_MODEL), row), pl.BlockSpec((tm, D_POOL), row),
                  pl.BlockSpec((tm, D_ATTN), row), pl.BlockSpec((tm, 2 * D_MODEL), row),
                  _resident(wg_b.shape), _resident((1, D_POOL)), _resident(wpp_b.shape),
                  _resident(wap_b.shape), _resident(wo_b.shape)],
        out_specs=pl.BlockSpec((tm, D_MODEL), row),
        out_shape=jax.ShapeDtypeStruct((m, D_MODEL), F32),
        compiler_params=_params("parallel"),
        name="mix",
    )(x2d, pooled, on, gates, wg_b, pool_scale, wpp_b, wap_b, wo_b)


MLP_CHUNK = 1024


def _rms(x, g):
    ms = jnp.mean(x * x, axis=-1, keepdims=True)
    return x * lax.rsqrt(ms + RMS_EPS) * g


def _mlp_kernel(x_ref, g_ref, wup_ref, wdn_ref, gf_ref, o_ref, h_scr):
    x = x_ref[...]
    h_scr[...] = _rms(x, g_ref[...]).astype(BF16)
    y = x
    for c in range(D_FF // MLP_CHUNK):
        sl = slice(c * MLP_CHUNK, (c + 1) * MLP_CHUNK)
        a = jnp.maximum(jnp.dot(h_scr[...], wup_ref[:, sl], preferred_element_type=F32), 0.0)
        y = y + jnp.dot((a * a).astype(BF16), wdn_ref[sl, :], preferred_element_type=F32)
    o_ref[...] = _rms(y, gf_ref[...])


def _mlp(x2d, g_mlp, wup_b, wdn_b, g_final, *, tm):
    m = x2d.shape[0]
    row = lambda i: (i, 0)
    return pl.pallas_call(
        _mlp_kernel,
        grid=(m // tm,),
        in_specs=[pl.BlockSpec((tm, D_MODEL), row), _resident((1, D_MODEL)), _resident(wup_b.shape),
                  _resident(wdn_b.shape), _resident((1, D_MODEL))],
        out_specs=pl.BlockSpec((tm, D_MODEL), row),
        out_shape=jax.ShapeDtypeStruct((m, D_MODEL), F32),
        scratch_shapes=[pltpu.VMEM((tm, D_MODEL), BF16)],
        compiler_params=_params("parallel"),
        name="mlp",
    )(x2d, g_mlp, wup_b, wdn_b, g_final)


PROMPT_TM = 512
ATTN_BLK = 512


def kernel(x_prompt, x_sample, state_pool, cache_k, cache_v, page_table, g_mix, w_in, w_pool_grp, pool_scale,
           w_pool_proj, lambda_q1, lambda_k1, lambda_q2, lambda_k2, g_subln, w_attn_proj, w_o, g_mlp, w_up,
           w_down, g_final):
    depth = w_in.shape[0]
    assert depth == 1, "single-layer trunk"
    layer = 0
    lam_init = _lambda_init(layer)
    b, t, _ = x_prompt.shape
    n, t_s, _ = x_sample.shape
    assert t_s == 1 and t % PROMPT_TM == 0 and t % ATTN_BLK == 0
    past_len = page_table.shape[1] * PAGE_SIZE

    w_in_b = w_in[layer].astype(BF16)
    wg_b = w_pool_grp[layer].astype(BF16)
    wpp_b = w_pool_proj[layer].astype(BF16)
    wap_b = w_attn_proj[layer].astype(BF16)
    wo_b = w_o[layer].astype(BF16)
    wup_b = w_up[layer].astype(BF16)
    wdn_b = w_down[layer].astype(BF16)
    g_mix2 = g_mix[layer][None, :]
    g_mlp2 = g_mlp[layer][None, :]
    g_fin2 = g_final[None, :]
    g_sub2 = g_subln[layer][None, :]
    scale2 = pool_scale[layer][None, :]

    cos_tab, sin_tab, lam = _prep(lambda_q1[layer], lambda_k1[layer], lambda_q2[layer], lambda_k2[layer],
                                  n_prompt=t, n_sample_rows=n, past_len=past_len, lam_init=lam_init)

    tiles_per_seq = t // PROMPT_TM
    xp2 = x_prompt.reshape(b * t, D_MODEL)
    u_p, q_p, k_p, v_p, kb_p, vb_p, gates_p = _proj(
        xp2, g_mix2, w_in_b, cos_tab, sin_tab, tm=PROMPT_TM, tab_block_of=lambda i: i % tiles_per_seq)
    u_p3 = u_p.reshape(b, t, D_POOL)
    pooled_p = _pool_prompt(u_p3, tm=PROMPT_TM).reshape(b * t, D_POOL)
    on_p = _attn_prompt(lam, g_sub2, q_p.reshape(b, t, D_ATTN), kb_p.reshape(b, t, D_ATTN),
                        vb_p.reshape(b, t, D_ATTN), blk=ATTN_BLK, lam_init=lam_init)
    x1_p = _mix(xp2, pooled_p, on_p.reshape(b * t, D_ATTN), gates_p, wg_b, scale2, wpp_b, wap_b, wo_b,
                tm=PROMPT_TM)
    y_p = _mlp(x1_p, g_mlp2, wup_b, wdn_b, g_fin2, tm=PROMPT_TM)

    xs2 = x_sample.reshape(n, D_MODEL)
    u_s, q_s, k_s, v_s, _, _, gates_s = _proj(
        xs2, g_mix2, w_in_b, cos_tab, sin_tab, tm=n, tab_block_of=lambda i: t // n)
    state_t = jnp.swapaxes(state_pool[layer], 0, 1)
    pooled_s = _pool_sample(state_t, u_s, past_len=past_len)
    heads = lambda a: a.reshape(n, N_HEADS, HEAD_W)
    on_s = _attn_sample(page_table, lam, g_sub2, heads(q_s), heads(k_s), heads(v_s), cache_k[layer], cache_v[layer],
                        lam_init=lam_init)
    x1_s = _mix(xs2, pooled_s, on_s.reshape(n, D_ATTN), gates_s, wg_b, scale2, wpp_b, wap_b, wo_b, tm=n)
    y_s = _mlp(x1_s, g_mlp2, wup_b, wdn_b, g_fin2, tm=n)

    y_prompt = y_p.reshape(b, t, D_MODEL)
    y_sample = y_s.reshape(n, 1, D_MODEL)
    new_k_prompt = k_p.reshape(1, b, t, N_HEADS, HEAD_W)
    new_v_prompt = v_p.reshape(1, b, t, N_HEADS, HEAD_W)
    new_pool_prompt = u_p3[:, t - POOL_BUF:, :][None]
    new_k_sample = k_s.reshape(1, n, 1, N_HEADS, HEAD_W)
    new_v_sample = v_s.reshape(1, n, 1, N_HEADS, HEAD_W)
    new_pool_sample = jnp.concatenate([state_pool[layer][:, 1:, :], u_s[:, None, :]], axis=1)[None]
    return (y_prompt, y_sample, new_k_prompt, new_v_prompt, new_pool_prompt,
            new_k_sample, new_v_sample, new_pool_sample)
```

```python
import functools
import math

import jax
import jax.numpy as jnp
from jax import lax
from jax.experimental import pallas as pl
from jax.experimental.pallas import tpu as pltpu

D_MODEL = 1024
N_HEADS = 8
HEAD_DIM = 64
HEAD_W = 2 * HEAD_DIM
D_ATTN = N_HEADS * HEAD_W
POOL_WINDOWS = (2, 4, 8, 16)
D_POOL = D_MODEL // 2
POOL_GROUP = D_POOL // len(POOL_WINDOWS)
POOL_BUF = max(POOL_WINDOWS) - 1
HALO = 16
D_FF = 4 * D_MODEL
ROPE_THETA = 10000.0
RMS_EPS = 1e-6
SUBLN_EPS = 1e-5
NEG_INF = -1e30
PAGE_SIZE = 128
OFF_Q = D_POOL
OFF_K = OFF_Q + D_ATTN
OFF_V = OFF_K + D_ATTN
OFF_G = OFF_V + D_ATTN
D_IN = OFF_G + 2 * D_MODEL

LANES = 128
SUBLANES = 8
VMEM_LIMIT = 56 * 1024 * 1024

BF16 = jnp.bfloat16
F32 = jnp.float32


def _lambda_init(layer):
    return 0.8 - 0.6 * math.exp(-0.3 * layer)


def _resident(shape):
    nd = len(shape)
    return pl.BlockSpec(shape, lambda *_: (0,) * nd, pipeline_mode=pl.Buffered(1))


def _params(*sem):
    return pltpu.CompilerParams(dimension_semantics=sem, vmem_limit_bytes=VMEM_LIMIT)


def _prep_kernel(invf_ref, lq1_ref, lk1_ref, lq2_ref, lk2_ref, cos_ref, sin_ref, lam_ref, *, n_prompt, past_len, lam_init):
    rows = cos_ref.shape[0]
    row = lax.broadcasted_iota(jnp.int32, (rows, LANES), 0)
    lane = lax.broadcasted_iota(jnp.int32, (rows, LANES), 1)
    pos = jnp.where(row < n_prompt, row, past_len).astype(F32)
    ang = pos * invf_ref[...]
    cos_ref[...] = jnp.cos(ang)
    sin_ref[...] = jnp.where((lane % HEAD_DIM) < HEAD_DIM // 2, -jnp.sin(ang), jnp.sin(ang))
    d1 = jnp.sum(lq1_ref[...] * lk1_ref[...], axis=-1, keepdims=True)
    d2 = jnp.sum(lq2_ref[...] * lk2_ref[...], axis=-1, keepdims=True)
    lam = jnp.exp(d1) - jnp.exp(d2) + lam_init
    lam_ref[...] = jnp.broadcast_to(lam, lam_ref.shape)


def _prep(lq1, lk1, lq2, lk2, *, n_prompt, n_sample_rows, past_len, lam_init):
    half = HEAD_DIM // 2
    inv_freq = ROPE_THETA ** (-jnp.arange(half, dtype=F32) * (2.0 / HEAD_DIM))
    invf = jnp.tile(inv_freq, LANES // half)[None, :]
    rows = n_prompt + n_sample_rows
    kern = functools.partial(_prep_kernel, n_prompt=n_prompt, past_len=past_len, lam_init=lam_init)
    return pl.pallas_call(
        kern,
        out_shape=(jax.ShapeDtypeStruct((rows, LANES), F32),
                   jax.ShapeDtypeStruct((rows, LANES), F32),
                   jax.ShapeDtypeStruct((SUBLANES, LANES), F32)),
        name="prep",
    )(invf, lq1[None, :], lk1[None, :], lq2[None, :], lk2[None, :])


PROJ_CHUNK = 512
Q_SCALE = HEAD_DIM ** -0.5 * math.log2(math.e)


def _rope(z, cos, sin, first_half):
    swapped = jnp.where(first_half, pltpu.roll(z, LANES - HEAD_DIM // 2, 1), pltpu.roll(z, HEAD_DIM // 2, 1))
    return z * cos + swapped * sin


def _proj_kernel(x_ref, g_ref, w_ref, cos_ref, sin_ref,
                 u_ref, q_ref, k_ref, v_ref, kb_ref, vb_ref, gate_ref, h_scr):
    x = x_ref[...]
    ms = jnp.mean(x * x, axis=-1, keepdims=True)
    h_scr[...] = (x * lax.rsqrt(ms + RMS_EPS) * g_ref[...]).astype(BF16)
    tm = x.shape[0]
    cos = cos_ref[...]
    sin = sin_ref[...]
    lane = lax.broadcasted_iota(jnp.int32, (tm, LANES), 1)
    first_half = (lane % HEAD_DIM) < HEAD_DIM // 2

    def dot(off, width):
        return jnp.dot(h_scr[...], w_ref[:, off:off + width], preferred_element_type=F32)

    u_ref[...] = dot(0, D_POOL)
    for c in range(D_ATTN // PROJ_CHUNK):
        zq = dot(OFF_Q + c * PROJ_CHUNK, PROJ_CHUNK)
        zk = dot(OFF_K + c * PROJ_CHUNK, PROJ_CHUNK)
        for j in range(PROJ_CHUNK // LANES):
            sl = slice(j * LANES, (j + 1) * LANES)
            osl = slice(c * PROJ_CHUNK + j * LANES, c * PROJ_CHUNK + (j + 1) * LANES)
            q_ref[:, osl] = (_rope(zq[:, sl], cos, sin, first_half) * Q_SCALE).astype(BF16)
            kr = _rope(zk[:, sl], cos, sin, first_half)
            k_ref[:, osl] = kr
            kb_ref[:, osl] = kr.astype(BF16)
        zv = dot(OFF_V + c * PROJ_CHUNK, PROJ_CHUNK)
        csl = slice(c * PROJ_CHUNK, (c + 1) * PROJ_CHUNK)
        v_ref[:, csl] = zv
        vb_ref[:, csl] = zv.astype(BF16)
    for c in range(2 * D_MODEL // PROJ_CHUNK):
        zg = dot(OFF_G + c * PROJ_CHUNK, PROJ_CHUNK)
        gate_ref[:, c * PROJ_CHUNK:(c + 1) * PROJ_CHUNK] = jax.nn.sigmoid(zg)


def _proj(x2d, g_mix, w_in_b, cos_tab, sin_tab, *, tm, tab_block_of):
    m = x2d.shape[0]
    row = lambda i: (i, 0)
    tab = pl.BlockSpec((tm, LANES), lambda i: (tab_block_of(i), 0))
    outs = (
        jax.ShapeDtypeStruct((m, D_POOL), F32),
        jax.ShapeDtypeStruct((m, D_ATTN), BF16),
        jax.ShapeDtypeStruct((m, D_ATTN), F32),
        jax.ShapeDtypeStruct((m, D_ATTN), F32),
        jax.ShapeDtypeStruct((m, D_ATTN), BF16),
        jax.ShapeDtypeStruct((m, D_ATTN), BF16),
        jax.ShapeDtypeStruct((m, 2 * D_MODEL), F32),
    )
    return pl.pallas_call(
        _proj_kernel,
        grid=(m // tm,),
        in_specs=[pl.BlockSpec((tm, D_MODEL), row), _resident((1, D_MODEL)), _resident((D_MODEL, D_IN)), tab, tab],
        out_specs=tuple(pl.BlockSpec((tm, s.shape[1]), row) for s in outs),
        out_shape=outs,
        scratch_shapes=[pltpu.VMEM((tm, D_MODEL), BF16)],
        compiler_params=_params("parallel"),
        name="proj",
    )(x2d, g_mix, w_in_b, cos_tab, sin_tab)


def _pool_prompt_kernel(u_ref, halo_ref, o_ref, ext_scr, *, tm):
    i = pl.program_id(1)
    ext_scr[0:HALO, :] = jnp.where(i > 0, halo_ref[...], 0.0)
    ext_scr[HALO:HALO + tm, :] = u_ref[...]
    pos = i * tm + lax.broadcasted_iota(jnp.int32, (tm, POOL_GROUP), 0)
    for g, w in enumerate(POOL_WINDOWS):
        sl = slice(g * POOL_GROUP, (g + 1) * POOL_GROUP)
        cur = ext_scr[HALO:HALO + tm, sl]
        acc = cur
        for k in range(1, w):
            acc = acc + ext_scr[HALO - k:HALO - k + tm, sl]
        count = jnp.minimum(pos + 1, w).astype(F32)
        o_ref[:, sl] = (acc / count - cur).astype(o_ref.dtype)


def _pool_prompt(u3d, *, tm):
    b, t, _ = u3d.shape
    kern = functools.partial(_pool_prompt_kernel, tm=tm)
    return pl.pallas_call(
        kern,
        grid=(b, t // tm),
        in_specs=[pl.BlockSpec((None, tm, D_POOL), lambda bi, i: (bi, i, 0)),
                  pl.BlockSpec((None, HALO, D_POOL), lambda bi, i: (bi, jnp.maximum(i * (tm // HALO) - 1, 0), 0))],
        out_specs=pl.BlockSpec((None, tm, D_POOL), lambda bi, i: (bi, i, 0)),
        out_shape=jax.ShapeDtypeStruct((b, t, D_POOL), BF16),
        scratch_shapes=[pltpu.VMEM((HALO + tm, D_POOL), F32)],
        compiler_params=_params("parallel", "parallel"),
        name="pool_prompt",
    )(u3d, u3d)


def _pool_sample_kernel(state_ref, u_ref, o_ref, *, past_len):
    for g, w in enumerate(POOL_WINDOWS):
        sl = slice(g * POOL_GROUP, (g + 1) * POOL_GROUP)
        cur = u_ref[:, sl]
        acc = cur
        for k in range(1, w):
            acc = acc + state_ref[POOL_BUF - k, :, sl]
        count = float(min(past_len + 1, w))
        o_ref[:, sl] = (acc / count - cur).astype(o_ref.dtype)


def _pool_sample(state_t, u_s, *, past_len):
    n = u_s.shape[0]
    return pl.pallas_call(
        functools.partial(_pool_sample_kernel, past_len=past_len),
        out_shape=jax.ShapeDtypeStruct((n, D_POOL), BF16),
        name="pool_sample",
    )(state_t, u_s)


def _subln(o, g, lam_init):
    ms = jnp.mean(o * o, axis=-1, keepdims=True)
    return o * lax.rsqrt(ms + SUBLN_EPS) * g * (1.0 - lam_init)


ONES_ROWS = 16
VT_ROWS = HEAD_W + ONES_ROWS
COL_STRIP = 256


def _attn_prompt_kernel(lam_ref, g_ref, q_ref, k_ref, v_ref, o_ref,
                        vt_scr, qt_scr, s0_scr, s1_scr, m_scr, acc_scr, *, blk, lam_init):
    nblk = k_ref.shape[0] // blk
    drow = lax.broadcasted_iota(jnp.int32, (HEAD_W, blk), 0)
    for r in range(nblk):
        rows = slice(r * blk, (r + 1) * blk)
        vt_scr[r, 0:HEAD_W, :] = v_ref[rows, :].astype(F32).T.astype(BF16)
        vt_scr[r, HEAD_W:VT_ROWS, :] = jnp.ones((ONES_ROWS, blk), BF16)
        qt = q_ref[rows, :].astype(F32).T
        qt_scr[r, 0] = jnp.where(drow < HEAD_DIM, qt, 0.0).astype(BF16)
        qt_scr[r, 1] = jnp.where(drow >= HEAD_DIM, qt, 0.0).astype(BF16)
    krow = lax.broadcasted_iota(jnp.int32, (blk, blk), 0)
    qcol = lax.broadcasted_iota(jnp.int32, (blk, blk), 1)
    causal = krow <= qcol
    lam = lam_ref[0:1, 0:1]
    gain = g_ref[...] * (1.0 - lam_init)
    s_bufs = (s0_scr, s1_scr)
    units = [(qi, j) for qi in range(nblk) for j in range(qi + 1)]

    def fill(u):
        qi, j = units[u]
        k = k_ref[j * blk:(j + 1) * blk, :]
        for c in range(2):
            s = jnp.dot(k, qt_scr[qi, c], preferred_element_type=F32)
            s_bufs[u % 2][c] = jnp.where(causal, s, NEG_INF) if j == qi else s

    def consume(u):
        qi, j = units[u]
        st = qi % 2
        vt = vt_scr[j]
        for c in range(2):
            for h in range(blk // COL_STRIP):
                cols = slice(h * COL_STRIP, (h + 1) * COL_STRIP)
                s_ref = s_bufs[u % 2].at[c, :, cols]
                m_blk = jnp.max(s_ref[...], axis=0, keepdims=True)
                if j == 0:
                    m_new = m_blk
                    p = jnp.exp2(s_ref[...] - m_new).astype(BF16)
                    acc_scr[st, c, :, cols] = jnp.dot(vt, p, preferred_element_type=F32)
                else:
                    m_prev = m_scr[st, c, :, cols]
                    m_new = jnp.maximum(m_prev, m_blk)
                    p = jnp.exp2(s_ref[...] - m_new).astype(BF16)
                    acc_scr[st, c, :, cols] = (jnp.exp2(m_prev - m_new) * acc_scr[st, c, :, cols]
                                               + jnp.dot(vt, p, preferred_element_type=F32))
                m_scr[st, c, :, cols] = m_new

    def finalize(qi):
        a1 = acc_scr[qi % 2, 0]
        a2 = acc_scr[qi % 2, 1]
        ot = (a1[0:HEAD_W] * (1.0 / a1[HEAD_W:HEAD_W + 1])
              - lam * (a2[0:HEAD_W] * (1.0 / a2[HEAD_W:HEAD_W + 1])))
        ms = jnp.mean(ot * ot, axis=0, keepdims=True)
        on = (ot * lax.rsqrt(ms + SUBLN_EPS)).T * gain
        o_ref[qi * blk:(qi + 1) * blk, :] = on.astype(o_ref.dtype)

    fill(0)
    for u, (qi, j) in enumerate(units):
        if u + 1 < len(units):
            fill(u + 1)
        consume(u)
        if j == qi:
            finalize(qi)


def _attn_prompt(lam, g_subln, q, kb, vb, *, blk, lam_init):
    b, t, _ = q.shape
    kern = functools.partial(_attn_prompt_kernel, blk=blk, lam_init=lam_init)
    seq = pl.BlockSpec((None, t, HEAD_W), lambda bi, h: (bi, 0, h))
    return pl.pallas_call(
        kern,
        grid=(b, N_HEADS),
        in_specs=[_resident((SUBLANES, LANES)), _resident((1, HEAD_W)), seq, seq, seq],
        out_specs=seq,
        out_shape=jax.ShapeDtypeStruct((b, t, D_ATTN), BF16),
        scratch_shapes=[pltpu.VMEM((t // blk, VT_ROWS, blk), BF16), pltpu.VMEM((t // blk, 2, HEAD_W, blk), BF16),
                        pltpu.VMEM((2, blk, blk), F32), pltpu.VMEM((2, blk, blk), F32),
                        pltpu.VMEM((2, 2, 1, blk), F32), pltpu.VMEM((2, 2, VT_ROWS, blk), F32)],
        compiler_params=_params("parallel", "parallel"),
        name="attn_prompt",
    )(lam, g_subln, q, kb, vb)


N_MAPROWS = 2 * N_HEADS
KEYS_PER_PAGE = PAGE_SIZE * N_HEADS


def _attn_sample_kernel(pt_ref, lam_ref, g_ref, q_ref, kn_ref, vn_ref, *refs, n_pages, lam_init):
    del pt_ref
    k_refs = refs[:n_pages]
    v_refs = refs[n_pages:2 * n_pages]
    o_ref = refs[2 * n_pages]

    row = lax.broadcasted_iota(jnp.int32, (N_MAPROWS, HEAD_W), 0)
    lane = lax.broadcasted_iota(jnp.int32, (N_MAPROWS, HEAD_W), 1)
    q = q_ref[...].astype(F32)
    qm = jnp.where((lane // HEAD_DIM) == (row // N_HEADS), jnp.concatenate([q, q], axis=0), 0.0).astype(BF16)

    krow = lax.broadcasted_iota(jnp.int32, (N_MAPROWS, KEYS_PER_PAGE), 0)
    kcol = lax.broadcasted_iota(jnp.int32, (N_MAPROWS, KEYS_PER_PAGE), 1)
    own = (kcol % N_HEADS) == (krow % N_HEADS)

    def scores(keys):
        return lax.dot_general(qm, keys, (((1,), (1,)), ((), ())), preferred_element_type=F32)

    s_pages = [jnp.where(own, scores(kr[...].reshape(KEYS_PER_PAGE, HEAD_W).astype(BF16)), NEG_INF)
               for kr in k_refs]
    pad = jnp.zeros((LANES - N_HEADS, HEAD_W), F32)
    kn = jnp.concatenate([kn_ref[...], pad], axis=0).astype(BF16)
    vn = jnp.concatenate([vn_ref[...], pad], axis=0).astype(BF16)
    s_new = jnp.where(lane == (row % N_HEADS), scores(kn), NEG_INF)

    m = jnp.max(s_new, axis=-1, keepdims=True)
    for s in s_pages:
        m = jnp.maximum(m, jnp.max(s, axis=-1, keepdims=True))
    p_new = jnp.exp2(s_new - m)
    l = jnp.sum(p_new, axis=-1, keepdims=True)
    acc = jnp.dot(p_new.astype(BF16), vn, preferred_element_type=F32)
    for s, vr in zip(s_pages, v_refs):
        p = jnp.exp2(s - m)
        l = l + jnp.sum(p, axis=-1, keepdims=True)
        acc = acc + jnp.dot(p.astype(BF16), vr[...].reshape(KEYS_PER_PAGE, HEAD_W).astype(BF16),
                            preferred_element_type=F32)
    a = acc / l
    o = a[0:N_HEADS] - lam_ref[0:1, 0:1] * a[N_HEADS:N_MAPROWS]
    o_ref[...] = _subln(o, g_ref[...], lam_init).astype(o_ref.dtype)


def _attn_sample(page_table, lam, g_subln, q3, kn3, vn3, cache_k4, cache_v4, *, lam_init):
    n, n_pages = page_table.shape
    pt_flat = page_table.reshape(-1)

    def page_spec(j):
        return pl.BlockSpec((None, PAGE_SIZE, N_HEADS, HEAD_W), lambda bi, pt: (pt[bi * n_pages + j], 0, 0, 0))

    seq = pl.BlockSpec((None, N_HEADS, HEAD_W), lambda bi, pt: (bi, 0, 0))
    const = lambda bi, pt: (0, 0)
    grid_spec = pltpu.PrefetchScalarGridSpec(
        num_scalar_prefetch=1,
        grid=(n,),
        in_specs=[pl.BlockSpec((SUBLANES, LANES), const), pl.BlockSpec((1, HEAD_W), const), seq, seq, seq]
                 + [page_spec(j) for j in range(n_pages)] * 2,
        out_specs=seq,
    )
    kern = functools.partial(_attn_sample_kernel, n_pages=n_pages, lam_init=lam_init)
    return pl.pallas_call(
        kern,
        grid_spec=grid_spec,
        out_shape=jax.ShapeDtypeStruct((n, N_HEADS, HEAD_W), F32),
        compiler_params=_params("parallel"),
        name="attn_sample",
    )(pt_flat, lam, g_subln, q3, kn3, vn3, *([cache_k4] * n_pages), *([cache_v4] * n_pages))


def _mix_kernel(x_ref, pooled_ref, on_ref, gate_ref, wg_ref, scale_ref, wpp_ref, wap_ref, wo_ref, o_ref):
    parts = []
    for g in range(len(POOL_WINDOWS)):
        sl = slice(g * POOL_GROUP, (g + 1) * POOL_GROUP)
        parts.append(jnp.dot(pooled_ref[:, sl], wg_ref[g], preferred_element_type=F32))
    pool_out = jnp.concatenate(parts, axis=1) * scale_ref[...]
    pool_proj = jnp.dot(pool_out.astype(BF16), wpp_ref[...], preferred_element_type=F32)
    attn_proj = jnp.dot(on_ref[...].astype(BF16), wap_ref[...], preferred_element_type=F32)
    merged = gate_ref[:, 0:D_MODEL] * pool_proj + gate_ref[:, D_MODEL:2 * D_MODEL] * attn_proj
    o_ref[...] = x_ref[...] + jnp.dot(merged.astype(BF16), wo_ref[...], preferred_element_type=F32)


def _mix(x2d, pooled, on, gates, wg_b, pool_scale, wpp_b, wap_b, wo_b, *, tm):
    m = x2d.shape[0]
    row = lambda i: (i, 0)
    return pl.pallas_call(
        _mix_kernel,
        grid=(m // tm,),
        in_specs=[pl.BlockSpec((tm, D_MODEL), row), pl.BlockSpec((tm, D_POOL), row),
                  pl.BlockSpec((tm, D_ATTN), row), pl.BlockSpec((tm, 2 * D_MODEL), row),
                  _resident(wg_b.shape), _resident((1, D_POOL)), _resident(wpp_b.shape),
                  _resident(wap_b.shape), _resident(wo_b.shape)],
        out_specs=pl.BlockSpec((tm, D_MODEL), row),
        out_shape=jax.ShapeDtypeStruct((m, D_MODEL), F32),
        compiler_params=_params("parallel"),
        name="mix",
    )(x2d, pooled, on, gates, wg_b, pool_scale, wpp_b, wap_b, wo_b)


MLP_CHUNK = 1024


def _rms(x, g):
    ms = jnp.mean(x * x, axis=-1, keepdims=True)
    return x * lax.rsqrt(ms + RMS_EPS) * g


def _mlp_kernel(x_ref, g_ref, wup_ref, wdn_ref, gf_ref, o_ref, h_scr):
    x = x_ref[...]
    h_scr[...] = _rms(x, g_ref[...]).astype(BF16)
    y = x
    for c in range(D_FF // MLP_CHUNK):
        sl = slice(c * MLP_CHUNK, (c + 1) * MLP_CHUNK)
        a = jnp.maximum(jnp.dot(h_scr[...], wup_ref[:, sl], preferred_element_type=F32), 0.0)
        y = y + jnp.dot((a * a).astype(BF16), wdn_ref[sl, :], preferred_element_type=F32)
    o_ref[...] = _rms(y, gf_ref[...])


def _mlp(x2d, g_mlp, wup_b, wdn_b, g_final, *, tm):
    m = x2d.shape[0]
    row = lambda i: (i, 0)
    return pl.pallas_call(
        _mlp_kernel,
        grid=(m // tm,),
        in_specs=[pl.BlockSpec((tm, D_MODEL), row), _resident((1, D_MODEL)), _resident(wup_b.shape),
                  _resident(wdn_b.shape), _resident((1, D_MODEL))],
        out_specs=pl.BlockSpec((tm, D_MODEL), row),
        out_shape=jax.ShapeDtypeStruct((m, D_MODEL), F32),
        scratch_shapes=[pltpu.VMEM((tm, D_MODEL), BF16)],
        compiler_params=_params("parallel"),
        name="mlp",
    )(x2d, g_mlp, wup_b, wdn_b, g_final)


PROMPT_TM = 512
ATTN_BLK = 512


def kernel(x_prompt, x_sample, state_pool, cache_k, cache_v, page_table, g_mix, w_in, w_pool_grp, pool_scale,
           w_pool_proj, lambda_q1, lambda_k1, lambda_q2, lambda_k2, g_subln, w_attn_proj, w_o, g_mlp, w_up,
           w_down, g_final):
    depth = w_in.shape[0]
    assert depth == 1, "single-layer trunk"
    layer = 0
    lam_init = _lambda_init(layer)
    b, t, _ = x_prompt.shape
    n, t_s, _ = x_sample.shape
    assert t_s == 1 and t % PROMPT_TM == 0 and t % ATTN_BLK == 0
    past_len = page_table.shape[1] * PAGE_SIZE

    w_in_b = w_in[layer].astype(BF16)
    wg_b = w_pool_grp[layer].astype(BF16)
    wpp_b = w_pool_proj[layer].astype(BF16)
    wap_b = w_attn_proj[layer].astype(BF16)
    wo_b = w_o[layer].astype(BF16)
    wup_b = w_up[layer].astype(BF16)
    wdn_b = w_down[layer].astype(BF16)
    g_mix2 = g_mix[layer][None, :]
    g_mlp2 = g_mlp[layer][None, :]
    g_fin2 = g_final[None, :]
    g_sub2 = g_subln[layer][None, :]
    scale2 = pool_scale[layer][None, :]

    cos_tab, sin_tab, lam = _prep(lambda_q1[layer], lambda_k1[layer], lambda_q2[layer], lambda_k2[layer],
                                  n_prompt=t, n_sample_rows=n, past_len=past_len, lam_init=lam_init)

    tiles_per_seq = t // PROMPT_TM
    xp2 = x_prompt.reshape(b * t, D_MODEL)
    u_p, q_p, k_p, v_p, kb_p, vb_p, gates_p = _proj(
        xp2, g_mix2, w_in_b, cos_tab, sin_tab, tm=PROMPT_TM, tab_block_of=lambda i: i % tiles_per_seq)
    u_p3 = u_p.reshape(b, t, D_POOL)
    pooled_p = _pool_prompt(u_p3, tm=PROMPT_TM).reshape(b * t, D_POOL)
    on_p = _attn_prompt(lam, g_sub2, q_p.reshape(b, t, D_ATTN), kb_p.reshape(b, t, D_ATTN),
                        vb_p.reshape(b, t, D_ATTN), blk=ATTN_BLK, lam_init=lam_init)
    x1_p = _mix(xp2, pooled_p, on_p.reshape(b * t, D_ATTN), gates_p, wg_b, scale2, wpp_b, wap_b, wo_b,
                tm=PROMPT_TM)
    y_p = _mlp(x1_p, g_mlp2, wup_b, wdn_b, g_fin2, tm=PROMPT_TM)

    xs2 = x_sample.reshape(n, D_MODEL)
    u_s, q_s, k_s, v_s, _, _, gates_s = _proj(
        xs2, g_mix2, w_in_b, cos_tab, sin_tab, tm=n, tab_block_of=lambda i: t // n)
    state_t = jnp.swapaxes(state_pool[layer], 0, 1)
    pooled_s = _pool_sample(state_t, u_s, past_len=past_len)
    heads = lambda a: a.reshape(n, N_HEADS, HEAD_W)
    on_s = _attn_sample(page_table, lam, g_sub2, heads(q_s), heads(k_s), heads(v_s), cache_k[layer], cache_v[layer],
                        lam_init=lam_init)
    x1_s = _mix(xs2, pooled_s, on_s.reshape(n, D_ATTN), gates_s, wg_b, scale2, wpp_b, wap_b, wo_b, tm=n)
    y_s = _mlp(x1_s, g_mlp2, wup_b, wdn_b, g_fin2, tm=n)

    y_prompt = y_p.reshape(b, t, D_MODEL)
    y_sample = y_s.reshape(n, 1, D_MODEL)
    new_k_prompt = k_p.reshape(1, b, t, N_HEADS, HEAD_W)
    new_v_prompt = v_p.reshape(1, b, t, N_HEADS, HEAD_W)
    new_pool_prompt = u_p3[:, t - POOL_BUF:, :][None]
    new_k_sample = k_s.reshape(1, n, 1, N_HEADS, HEAD_W)
    new_v_sample = v_s.reshape(1, n, 1, N_HEADS, HEAD_W)
    new_pool_sample = jnp.concatenate([state_pool[layer][:, 1:, :], u_s[:, None, :]], axis=1)[None]
    return (y_prompt, y_sample, new_k_prompt, new_v_prompt, new_pool_prompt,
            new_k_sample, new_v_sample, new_pool_sample)
```

```python
import functools
import math

import jax
import jax.numpy as jnp
from jax import lax
from jax.experimental import pallas as pl
from jax.experimental.pallas import tpu as pltpu

D_MODEL = 1024
N_HEADS = 8
HEAD_DIM = 64
HEAD_W = 2 * HEAD_DIM
D_ATTN = N_HEADS * HEAD_W
POOL_WINDOWS = (2, 4, 8, 16)
D_POOL = D_MODEL // 2
POOL_GROUP = D_POOL // len(POOL_WINDOWS)
POOL_BUF = max(POOL_WINDOWS) - 1
HALO = 16
D_FF = 4 * D_MODEL
ROPE_THETA = 10000.0
RMS_EPS = 1e-6
SUBLN_EPS = 1e-5
NEG_INF = -1e30
PAGE_SIZE = 128
OFF_Q = D_POOL
OFF_K = OFF_Q + D_ATTN
OFF_V = OFF_K + D_ATTN
OFF_G = OFF_V + D_ATTN
D_IN = OFF_G + 2 * D_MODEL

LANES = 128
SUBLANES = 8
VMEM_LIMIT = 56 * 1024 * 1024

BF16 = jnp.bfloat16
F32 = jnp.float32


def _lambda_init(layer):
    return 0.8 - 0.6 * math.exp(-0.3 * layer)


def _resident(shape):
    nd = len(shape)
    return pl.BlockSpec(shape, lambda *_: (0,) * nd, pipeline_mode=pl.Buffered(1))


def _params(*sem):
    return pltpu.CompilerParams(dimension_semantics=sem, vmem_limit_bytes=VMEM_LIMIT)


def _prep_kernel(invf_ref, lq1_ref, lk1_ref, lq2_ref, lk2_ref, cos_ref, sin_ref, lam_ref, *, n_prompt, past_len, lam_init):
    rows = cos_ref.shape[0]
    row = lax.broadcasted_iota(jnp.int32, (rows, LANES), 0)
    lane = lax.broadcasted_iota(jnp.int32, (rows, LANES), 1)
    pos = jnp.where(row < n_prompt, row, past_len).astype(F32)
    ang = pos * invf_ref[...]
    cos_ref[...] = jnp.cos(ang)
    sin_ref[...] = jnp.where((lane % HEAD_DIM) < HEAD_DIM // 2, -jnp.sin(ang), jnp.sin(ang))
    d1 = jnp.sum(lq1_ref[...] * lk1_ref[...], axis=-1, keepdims=True)
    d2 = jnp.sum(lq2_ref[...] * lk2_ref[...], axis=-1, keepdims=True)
    lam = jnp.exp(d1) - jnp.exp(d2) + lam_init
    lam_ref[...] = jnp.broadcast_to(lam, lam_ref.shape)


def _prep(lq1, lk1, lq2, lk2, *, n_prompt, n_sample_rows, past_len, lam_init):
    half = HEAD_DIM // 2
    inv_freq = ROPE_THETA ** (-jnp.arange(half, dtype=F32) * (2.0 / HEAD_DIM))
    invf = jnp.tile(inv_freq, LANES // half)[None, :]
    rows = n_prompt + n_sample_rows
    kern = functools.partial(_prep_kernel, n_prompt=n_prompt, past_len=past_len, lam_init=lam_init)
    return pl.pallas_call(
        kern,
        out_shape=(jax.ShapeDtypeStruct((rows, LANES), F32),
                   jax.ShapeDtypeStruct((rows, LANES), F32),
                   jax.ShapeDtypeStruct((SUBLANES, LANES), F32)),
        name="prep",
    )(invf, lq1[None, :], lk1[None, :], lq2[None, :], lk2[None, :])


PROJ_CHUNK = 512
Q_SCALE = HEAD_DIM ** -0.5 * math.log2(math.e)


def _rope(z, cos, sin, first_half):
    swapped = jnp.where(first_half, pltpu.roll(z, LANES - HEAD_DIM // 2, 1), pltpu.roll(z, HEAD_DIM // 2, 1))
    return z * cos + swapped * sin


def _proj_kernel(x_ref, g_ref, w_ref, cos_ref, sin_ref,
                 u_ref, q_ref, k_ref, v_ref, kb_ref, vb_ref, gate_ref, h_scr):
    x = x_ref[...]
    ms = jnp.mean(x * x, axis=-1, keepdims=True)
    h_scr[...] = (x * lax.rsqrt(ms + RMS_EPS) * g_ref[...]).astype(BF16)
    tm = x.shape[0]
    cos = cos_ref[...]
    sin = sin_ref[...]
    lane = lax.broadcasted_iota(jnp.int32, (tm, LANES), 1)
    first_half = (lane % HEAD_DIM) < HEAD_DIM // 2

    def dot(off, width):
        return jnp.dot(h_scr[...], w_ref[:, off:off + width], preferred_element_type=F32)

    u_ref[...] = dot(0, D_POOL)
    for c in range(D_ATTN // PROJ_CHUNK):
        zq = dot(OFF_Q + c * PROJ_CHUNK, PROJ_CHUNK)
        zk = dot(OFF_K + c * PROJ_CHUNK, PROJ_CHUNK)
        for j in range(PROJ_CHUNK // LANES):
            sl = slice(j * LANES, (j + 1) * LANES)
            osl = slice(c * PROJ_CHUNK + j * LANES, c * PROJ_CHUNK + (j + 1) * LANES)
            q_ref[:, osl] = (_rope(zq[:, sl], cos, sin, first_half) * Q_SCALE).astype(BF16)
            kr = _rope(zk[:, sl], cos, sin, first_half)
            k_ref[:, osl] = kr
            kb_ref[:, osl] = kr.astype(BF16)
        zv = dot(OFF_V + c * PROJ_CHUNK, PROJ_CHUNK)
        csl = slice(c * PROJ_CHUNK, (c + 1) * PROJ_CHUNK)
        v_ref[:, csl] = zv
        vb_ref[:, csl] = zv.astype(BF16)
    for c in range(2 * D_MODEL // PROJ_CHUNK):
        zg = dot(OFF_G + c * PROJ_CHUNK, PROJ_CHUNK)
        gate_ref[:, c * PROJ_CHUNK:(c + 1) * PROJ_CHUNK] = jax.nn.sigmoid(zg)


def _proj(x2d, g_mix, w_in_b, cos_tab, sin_tab, *, tm, tab_block_of):
    m = x2d.shape[0]
    row = lambda i: (i, 0)
    tab = pl.BlockSpec((tm, LANES), lambda i: (tab_block_of(i), 0))
    outs = (
        jax.ShapeDtypeStruct((m, D_POOL), F32),
        jax.ShapeDtypeStruct((m, D_ATTN), BF16),
        jax.ShapeDtypeStruct((m, D_ATTN), F32),
        jax.ShapeDtypeStruct((m, D_ATTN), F32),
        jax.ShapeDtypeStruct((m, D_ATTN), BF16),
        jax.ShapeDtypeStruct((m, D_ATTN), BF16),
        jax.ShapeDtypeStruct((m, 2 * D_MODEL), F32),
    )
    return pl.pallas_call(
        _proj_kernel,
        grid=(m // tm,),
        in_specs=[pl.BlockSpec((tm, D_MODEL), row), _resident((1, D_MODEL)), _resident((D_MODEL, D_IN)), tab, tab],
        out_specs=tuple(pl.BlockSpec((tm, s.shape[1]), row) for s in outs),
        out_shape=outs,
        scratch_shapes=[pltpu.VMEM((tm, D_MODEL), BF16)],
        compiler_params=_params("parallel"),
        name="proj",
    )(x2d, g_mix, w_in_b, cos_tab, sin_tab)


def _pool_prompt_kernel(u_ref, halo_ref, o_ref, ext_scr, *, tm):
    i = pl.program_id(1)
    ext_scr[0:HALO, :] = jnp.where(i > 0, halo_ref[...], 0.0)
    ext_scr[HALO:HALO + tm, :] = u_ref[...]
    pos = i * tm + lax.broadcasted_iota(jnp.int32, (tm, POOL_GROUP), 0)
    for g, w in enumerate(POOL_WINDOWS):
        sl = slice(g * POOL_GROUP, (g + 1) * POOL_GROUP)
        cur = ext_scr[HALO:HALO + tm, sl]
        acc = cur
        for k in range(1, w):
            acc = acc + ext_scr[HALO - k:HALO - k + tm, sl]
        count = jnp.minimum(pos + 1, w).astype(F32)
        o_ref[:, sl] = (acc / count - cur).astype(o_ref.dtype)


def _pool_prompt(u3d, *, tm):
    b, t, _ = u3d.shape
    kern = functools.partial(_pool_prompt_kernel, tm=tm)
    return pl.pallas_call(
        kern,
        grid=(b, t // tm),
        in_specs=[pl.BlockSpec((None, tm, D_POOL), lambda bi, i: (bi, i, 0)),
                  pl.BlockSpec((None, HALO, D_POOL), lambda bi, i: (bi, jnp.maximum(i * (tm // HALO) - 1, 0), 0))],
        out_specs=pl.BlockSpec((None, tm, D_POOL), lambda bi, i: (bi, i, 0)),
        out_shape=jax.ShapeDtypeStruct((b, t, D_POOL), BF16),
        scratch_shapes=[pltpu.VMEM((HALO + tm, D_POOL), F32)],
        compiler_params=_params("parallel", "parallel"),
        name="pool_prompt",
    )(u3d, u3d)


def _pool_sample_kernel(state_ref, u_ref, o_ref, *, past_len):
    for g, w in enumerate(POOL_WINDOWS):
        sl = slice(g * POOL_GROUP, (g + 1) * POOL_GROUP)
        cur = u_ref[:, sl]
        acc = cur
        for k in range(1, w):
            acc = acc + state_ref[POOL_BUF - k, :, sl]
        count = float(min(past_len + 1, w))
        o_ref[:, sl] = (acc / count - cur).astype(o_ref.dtype)


def _pool_sample(state_t, u_s, *, past_len):
    n = u_s.shape[0]
    return pl.pallas_call(
        functools.partial(_pool_sample_kernel, past_len=past_len),
        out_shape=jax.ShapeDtypeStruct((n, D_POOL), BF16),
        name="pool_sample",
    )(state_t, u_s)


def _subln(o, g, lam_init):
    ms = jnp.mean(o * o, axis=-1, keepdims=True)
    return o * lax.rsqrt(ms + SUBLN_EPS) * g * (1.0 - lam_init)


ONES_ROWS = 16
VT_ROWS = HEAD_W + ONES_ROWS
COL_STRIP = 256


def _attn_prompt_kernel(lam_ref, g_ref, q_ref, k_ref, v_ref, o_ref,
                        vt_scr, qt_scr, s0_scr, s1_scr, m_scr, acc_scr, *, blk, lam_init):
    nblk = k_ref.shape[0] // blk
    drow = lax.broadcasted_iota(jnp.int32, (HEAD_W, blk), 0)
    for r in range(nblk):
        rows = slice(r * blk, (r + 1) * blk)
        vt_scr[r, 0:HEAD_W, :] = v_ref[rows, :].astype(F32).T.astype(BF16)
        vt_scr[r, HEAD_W:VT_ROWS, :] = jnp.ones((ONES_ROWS, blk), BF16)
        qt = q_ref[rows, :].astype(F32).T
        qt_scr[r, 0] = jnp.where(drow < HEAD_DIM, qt, 0.0).astype(BF16)
        qt_scr[r, 1] = jnp.where(drow >= HEAD_DIM, qt, 0.0).astype(BF16)
    krow = lax.broadcasted_iota(jnp.int32, (blk, blk), 0)
    qcol = lax.broadcasted_iota(jnp.int32, (blk, blk), 1)
    causal = krow <= qcol
    lam = lam_ref[0:1, 0:1]
    gain = g_ref[...] * (1.0 - lam_init)
    s_bufs = (s0_scr, s1_scr)
    units = [(qi, j) for qi in range(nblk) for j in range(qi + 1)]

    def fill(u):
        qi, j = units[u]
        k = k_ref[j * blk:(j + 1) * blk, :]
        for c in range(2):
            s = jnp.dot(k, qt_scr[qi, c], preferred_element_type=F32)
            s_bufs[u % 2][c] = jnp.where(causal, s, NEG_INF) if j == qi else s

    def consume(u):
        qi, j = units[u]
        st = qi % 2
        vt = vt_scr[j]
        for c in range(2):
            for h in range(blk // COL_STRIP):
                cols = slice(h * COL_STRIP, (h + 1) * COL_STRIP)
                s_ref = s_bufs[u % 2].at[c, :, cols]
                m_blk = jnp.max(s_ref[...], axis=0, keepdims=True)
                if j == 0:
                    m_new = m_blk
                    p = jnp.exp2(s_ref[...] - m_new).astype(BF16)
                    acc_scr[st, c, :, cols] = jnp.dot(vt, p, preferred_element_type=F32)
                else:
                    m_prev = m_scr[st, c, :, cols]
                    m_new = jnp.maximum(m_prev, m_blk)
                    p = jnp.exp2(s_ref[...] - m_new).astype(BF16)
                    acc_scr[st, c, :, cols] = (jnp.exp2(m_prev - m_new) * acc_scr[st, c, :, cols]
                                               + jnp.dot(vt, p, preferred_element_type=F32))
                m_scr[st, c, :, cols] = m_new

    def finalize(qi):
        a1 = acc_scr[qi % 2, 0]
        a2 = acc_scr[qi % 2, 1]
        ot = (a1[0:HEAD_W] * (1.0 / a1[HEAD_W:HEAD_W + 1])
              - lam * (a2[0:HEAD_W] * (1.0 / a2[HEAD_W:HEAD_W + 1])))
        ms = jnp.mean(ot * ot, axis=0, keepdims=True)
        on = (ot * lax.rsqrt(ms + SUBLN_EPS)).T * gain
        o_ref[qi * blk:(qi + 1) * blk, :] = on.astype(o_ref.dtype)

    fill(0)
    for u, (qi, j) in enumerate(units):
        if u + 1 < len(units):
            fill(u + 1)
        consume(u)
        if j == qi:
            finalize(qi)


def _attn_prompt(lam, g_subln, q, kb, vb, *, blk, lam_init):
    b, t, _ = q.shape
    kern = functools.partial(_attn_prompt_kernel, blk=blk, lam_init=lam_init)
    seq = pl.BlockSpec((None, t, HEAD_W), lambda bi, h: (bi, 0, h))
    return pl.pallas_call(
        kern,
        grid=(b, N_HEADS),
        in_specs=[_resident((SUBLANES, LANES)), _resident((1, HEAD_W)), seq, seq, seq],
        out_specs=seq,
        out_shape=jax.ShapeDtypeStruct((b, t, D_ATTN), BF16),
        scratch_shapes=[pltpu.VMEM((t // blk, VT_ROWS, blk), BF16), pltpu.VMEM((t // blk, 2, HEAD_W, blk), BF16),
                        pltpu.VMEM((2, blk, blk), F32), pltpu.VMEM((2, blk, blk), F32),
                        pltpu.VMEM((2, 2, 1, blk), F32), pltpu.VMEM((2, 2, VT_ROWS, blk), F32)],
        compiler_params=_params("parallel", "parallel"),
        name="attn_prompt",
    )(lam, g_subln, q, kb, vb)


N_MAPROWS = 2 * N_HEADS
HALF_TOKENS = PAGE_SIZE // 2
HALF_KEYS = HALF_TOKENS * N_HEADS
QUARTER_TOKENS = PAGE_SIZE // 4
QUARTER_KEYS = QUARTER_TOKENS * N_HEADS


def _decode_weights(q_ref, kn_ref, k_refs):
    row = lax.broadcasted_iota(jnp.int32, (N_MAPROWS, HEAD_W), 0)
    lane = lax.broadcasted_iota(jnp.int32, (N_MAPROWS, HEAD_W), 1)
    q = q_ref[...].astype(F32)
    qm = jnp.where((lane >= HEAD_DIM) == (row >= N_HEADS), jnp.concatenate([q, q], axis=0), 0.0)
    zero = jnp.zeros_like(qm)
    qm2 = jnp.concatenate([jnp.concatenate([qm, zero], axis=1),
                           jnp.concatenate([zero, qm], axis=1)], axis=0).astype(BF16)

    krow = lax.broadcasted_iota(jnp.int32, (2 * N_MAPROWS, HALF_KEYS), 0)
    kcol = lax.broadcasted_iota(jnp.int32, (2 * N_MAPROWS, HALF_KEYS), 1)
    head_bits = N_HEADS - 1
    own = (kcol & head_bits) == (krow & head_bits)

    def scores(queries, keys):
        return lax.dot_general(queries, keys, (((1,), (1,)), ((), ())), preferred_element_type=F32)

    s_pages = [jnp.where(own, scores(qm2, _page_halves(kr)), NEG_INF) for kr in k_refs]
    s_new = jnp.where(lane == (row & head_bits), scores(qm.astype(BF16), _pad_new(kn_ref)), NEG_INF)

    m2 = jnp.max(s_pages[0], axis=-1, keepdims=True)
    for s in s_pages[1:]:
        m2 = jnp.maximum(m2, jnp.max(s, axis=-1, keepdims=True))
    m = jnp.maximum(jnp.maximum(m2[0:N_MAPROWS], m2[N_MAPROWS:2 * N_MAPROWS]), jnp.max(s_new, axis=-1, keepdims=True))
    mm = jnp.concatenate([m, m], axis=0)
    p_new = jnp.exp2(s_new - m)
    l = jnp.sum(p_new, axis=-1, keepdims=True)
    l2 = jnp.zeros((2 * N_MAPROWS, 1), F32)
    p_pages = []
    for s in s_pages:
        p = jnp.exp2(s - mm)
        l2 = l2 + jnp.sum(p, axis=-1, keepdims=True)
        p_pages.append(p.astype(BF16))
    l = l + l2[0:N_MAPROWS] + l2[N_MAPROWS:2 * N_MAPROWS]
    return p_pages, p_new.astype(BF16), l


def _page_halves(ref):
    a = ref[0:HALF_TOKENS].reshape(HALF_KEYS, HEAD_W)
    b = ref[HALF_TOKENS:PAGE_SIZE].reshape(HALF_KEYS, HEAD_W)
    return jnp.concatenate([a, b], axis=1).astype(BF16)


def _pad_new(ref):
    pad = jnp.zeros((LANES - N_HEADS, HEAD_W), F32)
    return jnp.concatenate([ref[...], pad], axis=0).astype(BF16)


def _page_quarters(ref):
    parts = [ref[i * QUARTER_TOKENS:(i + 1) * QUARTER_TOKENS].reshape(QUARTER_KEYS, HEAD_W) for i in range(4)]
    return jnp.concatenate(parts, axis=1).astype(BF16)


def _decode_output(weights, lam_ref, g_ref, vn_ref, v_refs, o_ref, *, lam_init):
    p_pages, p_new, l = weights
    acc = jnp.dot(p_new, _pad_new(vn_ref), preferred_element_type=F32)
    acc4 = jnp.zeros((4 * N_MAPROWS, 4 * HEAD_W), F32)
    for p, vr in zip(p_pages, v_refs):
        p4 = jnp.concatenate([p[0:N_MAPROWS, 0:QUARTER_KEYS], p[0:N_MAPROWS, QUARTER_KEYS:HALF_KEYS],
                              p[N_MAPROWS:2 * N_MAPROWS, 0:QUARTER_KEYS],
                              p[N_MAPROWS:2 * N_MAPROWS, QUARTER_KEYS:HALF_KEYS]], axis=0)
        acc4 = acc4 + jnp.dot(p4, _page_quarters(vr), preferred_element_type=F32)
    for i in range(4):
        acc = acc + acc4[i * N_MAPROWS:(i + 1) * N_MAPROWS, i * HEAD_W:(i + 1) * HEAD_W]
    a = acc / l
    o = a[0:N_HEADS] - lam_ref[0:1, 0:1] * a[N_HEADS:N_MAPROWS]
    o_ref[...] = _subln(o, g_ref[...], lam_init).astype(o_ref.dtype)


def _mix_kernel(x_ref, pooled_ref, on_ref, gate_ref, wg_ref, scale_ref, wpp_ref, wap_ref, wo_ref, o_ref):
    parts = []
    for g in range(len(POOL_WINDOWS)):
        sl = slice(g * POOL_GROUP, (g + 1) * POOL_GROUP)
        parts.append(jnp.dot(pooled_ref[:, sl], wg_ref[g], preferred_element_type=F32))
    pool_out = jnp.concatenate(parts, axis=1) * scale_ref[...]
    pool_proj = jnp.dot(pool_out.astype(BF16), wpp_ref[...], preferred_element_type=F32)
    attn_proj = jnp.dot(on_ref[...].astype(BF16), wap_ref[...], preferred_element_type=F32)
    merged = gate_ref[:, 0:D_MODEL] * pool_proj + gate_ref[:, D_MODEL:2 * D_MODEL] * attn_proj
    o_ref[...] = x_ref[...] + jnp.dot(merged.astype(BF16), wo_ref[...], preferred_element_type=F32)


def _mix(x2d, pooled, on, gates, wg_b, pool_scale, wpp_b, wap_b, wo_b, *, tm):
    m = x2d.shape[0]
    row = lambda i: (i, 0)
    return pl.pallas_call(
        _mix_kernel,
        grid=(m // tm,),
        in_specs=[pl.BlockSpec((tm, D_MODEL), row), pl.BlockSpec((tm, D_POOL), row),
                  pl.BlockSpec((tm, D_ATTN), row), pl.BlockSpec((tm, 2 * D_MODEL), row),
                  _resident(wg_b.shape), _resident((1, D_POOL)), _resident(wpp_b.shape),
                  _resident(wap_b.shape), _resident(wo_b.shape)],
        out_specs=pl.BlockSpec((tm, D_MODEL), row),
        out_shape=jax.ShapeDtypeStruct((m, D_MODEL), F32),
        compiler_params=_params("parallel"),
        name="mix",
    )(x2d, pooled, on, gates, wg_b, pool_scale, wpp_b, wap_b, wo_b)


MLP_CHUNK = 1024


def _rms(x, g):
    ms = jnp.mean(x * x, axis=-1, keepdims=True)
    return x * lax.rsqrt(ms + RMS_EPS) * g


def _mlp_kernel(x_ref, g_ref, wup_ref, wdn_ref, gf_ref, o_ref, h_scr):
    x = x_ref[...]
    h_scr[...] = _rms(x, g_ref[...]).astype(BF16)
    y = x
    for c in range(D_FF // MLP_CHUNK):
        sl = slice(c * MLP_CHUNK, (c + 1) * MLP_CHUNK)
        a = jnp.maximum(jnp.dot(h_scr[...], wup_ref[:, sl], preferred_element_type=F32), 0.0)
        y = y + jnp.dot((a * a).astype(BF16), wdn_ref[sl, :], preferred_element_type=F32)
    o_ref[...] = _rms(y, gf_ref[...])


def _mlp(x2d, g_mlp, wup_b, wdn_b, g_final, *, tm):
    m = x2d.shape[0]
    row = lambda i: (i, 0)
    return pl.pallas_call(
        _mlp_kernel,
        grid=(m // tm,),
        in_specs=[pl.BlockSpec((tm, D_MODEL), row), _resident((1, D_MODEL)), _resident(wup_b.shape),
                  _resident(wdn_b.shape), _resident((1, D_MODEL))],
        out_specs=pl.BlockSpec((tm, D_MODEL), row),
        out_shape=jax.ShapeDtypeStruct((m, D_MODEL), F32),
        scratch_shapes=[pltpu.VMEM((tm, D_MODEL), BF16)],
        compiler_params=_params("parallel"),
        name="mlp",
    )(x2d, g_mlp, wup_b, wdn_b, g_final)


N_MLP_IN = 5
N_DEC_IN = 5


def _mlp_decode_kernel(pt_ref, *refs, n_pages, lam_init):
    del pt_ref
    mlp_in = refs[:N_MLP_IN]
    lam_ref, gs_ref, q_ref, kn_ref, vn_ref = refs[N_MLP_IN:N_MLP_IN + N_DEC_IN]
    pages = refs[N_MLP_IN + N_DEC_IN:N_MLP_IN + N_DEC_IN + 2 * n_pages]
    y_ref, on_ref, h_scr = refs[N_MLP_IN + N_DEC_IN + 2 * n_pages:]
    weights = _decode_weights(q_ref, kn_ref, pages[:n_pages])
    _mlp_kernel(*mlp_in, y_ref, h_scr)
    _decode_output(weights, lam_ref, gs_ref, vn_ref, pages[n_pages:], on_ref, lam_init=lam_init)


def _mlp_decode(x2d, g_mlp, wup_b, wdn_b, g_final, page_table, lam, g_subln, q3, kn3, vn3, cache_k4, cache_v4,
                *, lam_init):
    m = x2d.shape[0]
    n, n_pages = page_table.shape
    tm = m // n
    assert tm * n == m and tm % SUBLANES == 0
    pt_flat = page_table.reshape(-1)
    const2 = lambda i, pt: (0, 0)
    row = pl.BlockSpec((tm, D_MODEL), lambda i, pt: (i, 0))
    seq = pl.BlockSpec((None, N_HEADS, HEAD_W), lambda i, pt: (i, 0, 0))

    def whole(shape):
        return pl.BlockSpec(shape, const2, pipeline_mode=pl.Buffered(1))

    def page_spec(j):
        return pl.BlockSpec((None, PAGE_SIZE, N_HEADS, HEAD_W), lambda i, pt: (pt[i * n_pages + j], 0, 0, 0))

    grid_spec = pltpu.PrefetchScalarGridSpec(
        num_scalar_prefetch=1,
        grid=(n,),
        in_specs=[row, whole((1, D_MODEL)), whole(wup_b.shape), whole(wdn_b.shape), whole((1, D_MODEL)),
                  whole((SUBLANES, LANES)), whole((1, HEAD_W)), seq, seq, seq]
                 + [page_spec(j) for j in range(n_pages)] * 2,
        out_specs=(row, seq),
        scratch_shapes=[pltpu.VMEM((tm, D_MODEL), BF16)],
    )
    kern = functools.partial(_mlp_decode_kernel, n_pages=n_pages, lam_init=lam_init)
    return pl.pallas_call(
        kern,
        grid_spec=grid_spec,
        out_shape=(jax.ShapeDtypeStruct((m, D_MODEL), F32), jax.ShapeDtypeStruct((n, N_HEADS, HEAD_W), F32)),
        compiler_params=_params("parallel"),
        name="mlp_decode",
    )(pt_flat, x2d, g_mlp, wup_b, wdn_b, g_final, lam, g_subln, q3, kn3, vn3,
      *([cache_k4] * n_pages), *([cache_v4] * n_pages))


PROMPT_TM = 512
ATTN_BLK = 512


def kernel(x_prompt, x_sample, state_pool, cache_k, cache_v, page_table, g_mix, w_in, w_pool_grp, pool_scale,
           w_pool_proj, lambda_q1, lambda_k1, lambda_q2, lambda_k2, g_subln, w_attn_proj, w_o, g_mlp, w_up,
           w_down, g_final):
    depth = w_in.shape[0]
    assert depth == 1, "single-layer trunk"
    layer = 0
    lam_init = _lambda_init(layer)
    b, t, _ = x_prompt.shape
    n, t_s, _ = x_sample.shape
    assert t_s == 1 and t % PROMPT_TM == 0 and t % ATTN_BLK == 0
    past_len = page_table.shape[1] * PAGE_SIZE

    w_in_b = w_in[layer].astype(BF16)
    wg_b = w_pool_grp[layer].astype(BF16)
    wpp_b = w_pool_proj[layer].astype(BF16)
    wap_b = w_attn_proj[layer].astype(BF16)
    wo_b = w_o[layer].astype(BF16)
    wup_b = w_up[layer].astype(BF16)
    wdn_b = w_down[layer].astype(BF16)
    g_mix2 = g_mix[layer][None, :]
    g_mlp2 = g_mlp[layer][None, :]
    g_fin2 = g_final[None, :]
    g_sub2 = g_subln[layer][None, :]
    scale2 = pool_scale[layer][None, :]

    cos_tab, sin_tab, lam = _prep(lambda_q1[layer], lambda_k1[layer], lambda_q2[layer], lambda_k2[layer],
                                  n_prompt=t, n_sample_rows=n, past_len=past_len, lam_init=lam_init)

    tiles_per_seq = t // PROMPT_TM
    xp2 = x_prompt.reshape(b * t, D_MODEL)
    u_p, q_p, k_p, v_p, kb_p, vb_p, gates_p = _proj(
        xp2, g_mix2, w_in_b, cos_tab, sin_tab, tm=PROMPT_TM, tab_block_of=lambda i: i % tiles_per_seq)
    xs2 = x_sample.reshape(n, D_MODEL)
    u_s, q_s, k_s, v_s, _, _, gates_s = _proj(
        xs2, g_mix2, w_in_b, cos_tab, sin_tab, tm=n, tab_block_of=lambda i: t // n)

    u_p3 = u_p.reshape(b, t, D_POOL)
    pooled_p = _pool_prompt(u_p3, tm=PROMPT_TM).reshape(b * t, D_POOL)
    on_p = _attn_prompt(lam, g_sub2, q_p.reshape(b, t, D_ATTN), kb_p.reshape(b, t, D_ATTN),
                        vb_p.reshape(b, t, D_ATTN), blk=ATTN_BLK, lam_init=lam_init)
    x1_p = _mix(xp2, pooled_p, on_p.reshape(b * t, D_ATTN), gates_p, wg_b, scale2, wpp_b, wap_b, wo_b,
                tm=PROMPT_TM)

    heads = lambda a: a.reshape(n, N_HEADS, HEAD_W)
    y_p, on_s = _mlp_decode(x1_p, g_mlp2, wup_b, wdn_b, g_fin2, page_table, lam, g_sub2,
                            heads(q_s), heads(k_s), heads(v_s), cache_k[layer], cache_v[layer], lam_init=lam_init)

    state_t = jnp.swapaxes(state_pool[layer], 0, 1)
    pooled_s = _pool_sample(state_t, u_s, past_len=past_len)
    x1_s = _mix(xs2, pooled_s, on_s.reshape(n, D_ATTN), gates_s, wg_b, scale2, wpp_b, wap_b, wo_b, tm=n)
    y_s = _mlp(x1_s, g_mlp2, wup_b, wdn_b, g_fin2, tm=n)

    y_prompt = y_p.reshape(b, t, D_MODEL)
    y_sample = y_s.reshape(n, 1, D_MODEL)
    new_k_prompt = k_p.reshape(1, b, t, N_HEADS, HEAD_W)
    new_v_prompt = v_p.reshape(1, b, t, N_HEADS, HEAD_W)
    new_pool_prompt = u_p3[:, t - POOL_BUF:, :][None]
    new_k_sample = k_s.reshape(1, n, 1, N_HEADS, HEAD_W)
    new_v_sample = v_s.reshape(1, n, 1, N_HEADS, HEAD_W)
    new_pool_sample = jnp.concatenate([state_pool[layer][:, 1:, :], u_s[:, None, :]], axis=1)[None]
    return (y_prompt, y_sample, new_k_prompt, new_v_prompt, new_pool_prompt,
            new_k_sample, new_v_sample, new_pool_sample)
```

```python
import functools
import math

import jax
import jax.numpy as jnp
from jax import lax
from jax.experimental import pallas as pl
from jax.experimental.pallas import tpu as pltpu

D_MODEL = 1024
N_HEADS = 8
HEAD_DIM = 64
HEAD_W = 2 * HEAD_DIM
D_ATTN = N_HEADS * HEAD_W
POOL_WINDOWS = (2, 4, 8, 16)
D_POOL = D_MODEL // 2
POOL_GROUP = D_POOL // len(POOL_WINDOWS)
POOL_BUF = max(POOL_WINDOWS) - 1
HALO = 16
D_FF = 4 * D_MODEL
ROPE_THETA = 10000.0
RMS_EPS = 1e-6
SUBLN_EPS = 1e-5
NEG_INF = -1e30
PAGE_SIZE = 128
OFF_Q = D_POOL
OFF_K = OFF_Q + D_ATTN
OFF_V = OFF_K + D_ATTN
OFF_G = OFF_V + D_ATTN
D_IN = OFF_G + 2 * D_MODEL

LANES = 128
SUBLANES = 8
VMEM_LIMIT = 56 * 1024 * 1024

BF16 = jnp.bfloat16
F32 = jnp.float32


def _lambda_init(layer):
    return 0.8 - 0.6 * math.exp(-0.3 * layer)


def _resident(shape):
    nd = len(shape)
    return pl.BlockSpec(shape, lambda *_: (0,) * nd, pipeline_mode=pl.Buffered(1))


def _params(*sem):
    return pltpu.CompilerParams(dimension_semantics=sem, vmem_limit_bytes=VMEM_LIMIT)


def _prep_kernel(invf_ref, lq1_ref, lk1_ref, lq2_ref, lk2_ref, cos_ref, sin_ref, lam_ref, *, n_prompt, past_len, lam_init):
    rows = cos_ref.shape[0]
    row = lax.broadcasted_iota(jnp.int32, (rows, LANES), 0)
    lane = lax.broadcasted_iota(jnp.int32, (rows, LANES), 1)
    pos = jnp.where(row < n_prompt, row, past_len).astype(F32)
    ang = pos * invf_ref[...]
    cos_ref[...] = jnp.cos(ang)
    sin_ref[...] = jnp.where((lane % HEAD_DIM) < HEAD_DIM // 2, -jnp.sin(ang), jnp.sin(ang))
    d1 = jnp.sum(lq1_ref[...] * lk1_ref[...], axis=-1, keepdims=True)
    d2 = jnp.sum(lq2_ref[...] * lk2_ref[...], axis=-1, keepdims=True)
    lam = jnp.exp(d1) - jnp.exp(d2) + lam_init
    lam_ref[...] = jnp.broadcast_to(lam, lam_ref.shape)


def _prep(lq1, lk1, lq2, lk2, *, n_prompt, n_sample_rows, past_len, lam_init):
    half = HEAD_DIM // 2
    inv_freq = ROPE_THETA ** (-jnp.arange(half, dtype=F32) * (2.0 / HEAD_DIM))
    invf = jnp.tile(inv_freq, LANES // half)[None, :]
    rows = n_prompt + n_sample_rows
    kern = functools.partial(_prep_kernel, n_prompt=n_prompt, past_len=past_len, lam_init=lam_init)
    return pl.pallas_call(
        kern,
        out_shape=(jax.ShapeDtypeStruct((rows, LANES), F32),
                   jax.ShapeDtypeStruct((rows, LANES), F32),
                   jax.ShapeDtypeStruct((SUBLANES, LANES), F32)),
        name="prep",
    )(invf, lq1[None, :], lk1[None, :], lq2[None, :], lk2[None, :])


PROJ_CHUNK = 512
Q_SCALE = HEAD_DIM ** -0.5 * math.log2(math.e)


def _rope(z, cos, sin, first_half):
    swapped = jnp.where(first_half, pltpu.roll(z, LANES - HEAD_DIM // 2, 1), pltpu.roll(z, HEAD_DIM // 2, 1))
    return z * cos + swapped * sin


POOL_ROWS = 64


def _window_means(ext_scr, pos, o_ref, tm):
    for g, w in enumerate(POOL_WINDOWS):
        sl = slice(g * POOL_GROUP, (g + 1) * POOL_GROUP)
        for r0 in range(0, tm, POOL_ROWS):
            cur = ext_scr[HALO + r0:HALO + r0 + POOL_ROWS, sl]
            acc = cur
            for k in range(1, w):
                acc = acc + ext_scr[HALO + r0 - k:HALO + r0 - k + POOL_ROWS, sl]
            count = jnp.minimum(pos[r0:r0 + POOL_ROWS] + 1, w).astype(F32)
            o_ref[r0:r0 + POOL_ROWS, sl] = (acc / count - cur).astype(o_ref.dtype)


def _proj_kernel(x_ref, g_ref, w_ref, cos_ref, sin_ref,
                 u_ref, q_ref, k_ref, v_ref, kb_ref, vb_ref, gate_ref, *rest, tiles_per_seq):
    if tiles_per_seq is None:
        (h_scr,) = rest
    else:
        pooled_ref, h_scr, ext_scr = rest
    tm = x_ref.shape[0]
    if tiles_per_seq is not None:
        i = pl.program_id(0) % tiles_per_seq

        @pl.when(i == 0)
        def _():
            ext_scr[0:HALO, :] = jnp.zeros((HALO, D_POOL), F32)

        @pl.when(i > 0)
        def _():
            ext_scr[0:HALO, :] = ext_scr[tm:tm + HALO, :]

    x = x_ref[...]
    ms = jnp.mean(x * x, axis=-1, keepdims=True)
    h_scr[...] = (x * lax.rsqrt(ms + RMS_EPS) * g_ref[...]).astype(BF16)
    cos = cos_ref[...]
    sin = sin_ref[...]
    lane = lax.broadcasted_iota(jnp.int32, (tm, LANES), 1)
    first_half = (lane % HEAD_DIM) < HEAD_DIM // 2

    def dot(off, width):
        return jnp.dot(h_scr[...], w_ref[:, off:off + width], preferred_element_type=F32)

    u = dot(0, D_POOL)
    u_ref[...] = u
    if tiles_per_seq is not None:
        ext_scr[HALO:HALO + tm, :] = u
    for c in range(D_ATTN // PROJ_CHUNK):
        zq = dot(OFF_Q + c * PROJ_CHUNK, PROJ_CHUNK)
        zk = dot(OFF_K + c * PROJ_CHUNK, PROJ_CHUNK)
        for j in range(PROJ_CHUNK // LANES):
            sl = slice(j * LANES, (j + 1) * LANES)
            osl = slice(c * PROJ_CHUNK + j * LANES, c * PROJ_CHUNK + (j + 1) * LANES)
            q_ref[:, osl] = (_rope(zq[:, sl], cos, sin, first_half) * Q_SCALE).astype(BF16)
            kr = _rope(zk[:, sl], cos, sin, first_half)
            k_ref[:, osl] = kr
            kb_ref[:, osl] = kr.astype(BF16)
        zv = dot(OFF_V + c * PROJ_CHUNK, PROJ_CHUNK)
        csl = slice(c * PROJ_CHUNK, (c + 1) * PROJ_CHUNK)
        v_ref[:, csl] = zv
        vb_ref[:, csl] = zv.astype(BF16)
    if tiles_per_seq is not None:
        _window_means(ext_scr, i * tm + lax.broadcasted_iota(jnp.int32, (tm, POOL_GROUP), 0), pooled_ref, tm)
    for c in range(2 * D_MODEL // PROJ_CHUNK):
        zg = dot(OFF_G + c * PROJ_CHUNK, PROJ_CHUNK)
        gate_ref[:, c * PROJ_CHUNK:(c + 1) * PROJ_CHUNK] = jax.nn.sigmoid(zg)


def _proj(x2d, g_mix, w_in_b, cos_tab, sin_tab, *, tm, tab_block_of, tiles_per_seq=None):
    m = x2d.shape[0]
    row = lambda i: (i, 0)
    tab = pl.BlockSpec((tm, LANES), lambda i: (tab_block_of(i), 0))
    outs = [
        jax.ShapeDtypeStruct((m, D_POOL), F32),
        jax.ShapeDtypeStruct((m, D_ATTN), BF16),
        jax.ShapeDtypeStruct((m, D_ATTN), F32),
        jax.ShapeDtypeStruct((m, D_ATTN), F32),
        jax.ShapeDtypeStruct((m, D_ATTN), BF16),
        jax.ShapeDtypeStruct((m, D_ATTN), BF16),
        jax.ShapeDtypeStruct((m, 2 * D_MODEL), F32),
    ]
    scratch = [pltpu.VMEM((tm, D_MODEL), BF16)]
    if tiles_per_seq is not None:
        outs.append(jax.ShapeDtypeStruct((m, D_POOL), BF16))
        scratch.append(pltpu.VMEM((HALO + tm, D_POOL), F32))
    return pl.pallas_call(
        functools.partial(_proj_kernel, tiles_per_seq=tiles_per_seq),
        grid=(m // tm,),
        in_specs=[pl.BlockSpec((tm, D_MODEL), row), _resident((1, D_MODEL)), _resident((D_MODEL, D_IN)), tab, tab],
        out_specs=tuple(pl.BlockSpec((tm, s.shape[1]), row) for s in outs),
        out_shape=tuple(outs),
        scratch_shapes=scratch,
        compiler_params=_params("parallel" if tiles_per_seq is None else "arbitrary"),
        name="proj",
    )(x2d, g_mix, w_in_b, cos_tab, sin_tab)


def _pool_sample_kernel(state_ref, u_ref, o_ref, *, past_len):
    for g, w in enumerate(POOL_WINDOWS):
        sl = slice(g * POOL_GROUP, (g + 1) * POOL_GROUP)
        cur = u_ref[:, sl]
        acc = cur
        for k in range(1, w):
            acc = acc + state_ref[POOL_BUF - k, :, sl]
        count = float(min(past_len + 1, w))
        o_ref[:, sl] = (acc / count - cur).astype(o_ref.dtype)


def _pool_sample(state_t, u_s, *, past_len):
    n = u_s.shape[0]
    return pl.pallas_call(
        functools.partial(_pool_sample_kernel, past_len=past_len),
        out_shape=jax.ShapeDtypeStruct((n, D_POOL), BF16),
        name="pool_sample",
    )(state_t, u_s)


def _subln(o, g, lam_init):
    ms = jnp.mean(o * o, axis=-1, keepdims=True)
    return o * lax.rsqrt(ms + SUBLN_EPS) * g * (1.0 - lam_init)


ONES_ROWS = 16
VT_ROWS = HEAD_W + ONES_ROWS
COL_STRIP = 256


def _attn_prompt_kernel(lam_ref, g_ref, q_ref, k_ref, v_ref, o_ref,
                        vt_scr, qt_scr, s0_scr, s1_scr, m_scr, acc_scr, *, blk, lam_init):
    nblk = k_ref.shape[0] // blk
    drow = lax.broadcasted_iota(jnp.int32, (HEAD_W, blk), 0)
    for r in range(nblk):
        rows = slice(r * blk, (r + 1) * blk)
        vt_scr[r, 0:HEAD_W, :] = v_ref[rows, :].astype(F32).T.astype(BF16)
        vt_scr[r, HEAD_W:VT_ROWS, :] = jnp.ones((ONES_ROWS, blk), BF16)
        qt = q_ref[rows, :].astype(F32).T
        qt_scr[r, 0] = jnp.where(drow < HEAD_DIM, qt, 0.0).astype(BF16)
        qt_scr[r, 1] = jnp.where(drow >= HEAD_DIM, qt, 0.0).astype(BF16)
    lam = lam_ref[0:1, 0:1]
    gain = g_ref[...] * (1.0 - lam_init)
    s_bufs = (s0_scr, s1_scr)
    units = [(qi, j) for qi in range(nblk) for j in range(qi + 1)]
    strips = [slice(h * COL_STRIP, (h + 1) * COL_STRIP) for h in range(blk // COL_STRIP)]

    def n_keys(qi, j, h):
        return (h + 1) * COL_STRIP if j == qi else blk

    def fill(u):
        qi, j = units[u]
        for c in range(2):
            if j != qi:
                s_bufs[u % 2][c] = jnp.dot(k_ref[j * blk:(j + 1) * blk, :], qt_scr[qi, c],
                                           preferred_element_type=F32)
                continue
            for h, cols in enumerate(strips):
                nk = n_keys(qi, j, h)
                s = jnp.dot(k_ref[j * blk:j * blk + nk, :], qt_scr[qi, c, :, cols], preferred_element_type=F32)
                krow = lax.broadcasted_iota(jnp.int32, (nk, COL_STRIP), 0)
                qcol = lax.broadcasted_iota(jnp.int32, (nk, COL_STRIP), 1) + h * COL_STRIP
                s_bufs[u % 2][c, 0:nk, cols] = jnp.where(krow <= qcol, s, NEG_INF)

    def consume(u):
        qi, j = units[u]
        st = qi % 2
        for c in range(2):
            for h, cols in enumerate(strips):
                nk = n_keys(qi, j, h)
                vt = vt_scr[j, :, 0:nk]
                s_ref = s_bufs[u % 2].at[c, 0:nk, cols]
                m_blk = jnp.max(s_ref[...], axis=0, keepdims=True)
                if j == 0:
                    m_new = m_blk
                    p = jnp.exp2(s_ref[...] - m_new).astype(BF16)
                    acc_scr[st, c, :, cols] = jnp.dot(vt, p, preferred_element_type=F32)
                else:
                    m_prev = m_scr[st, c, :, cols]
                    m_new = jnp.maximum(m_prev, m_blk)
                    p = jnp.exp2(s_ref[...] - m_new).astype(BF16)
                    acc_scr[st, c, :, cols] = (jnp.exp2(m_prev - m_new) * acc_scr[st, c, :, cols]
                                               + jnp.dot(vt, p, preferred_element_type=F32))
                m_scr[st, c, :, cols] = m_new

    def finalize(qi):
        a1 = acc_scr[qi % 2, 0]
        a2 = acc_scr[qi % 2, 1]
        ot = (a1[0:HEAD_W] * (1.0 / a1[HEAD_W:HEAD_W + 1])
              - lam * (a2[0:HEAD_W] * (1.0 / a2[HEAD_W:HEAD_W + 1])))
        ms = jnp.mean(ot * ot, axis=0, keepdims=True)
        on = (ot * lax.rsqrt(ms + SUBLN_EPS)).T * gain
        o_ref[qi * blk:(qi + 1) * blk, :] = on.astype(o_ref.dtype)

    fill(0)
    for u, (qi, j) in enumerate(units):
        if u + 1 < len(units):
            fill(u + 1)
        consume(u)
        if j == qi:
            finalize(qi)


def _attn_prompt(lam, g_subln, q, kb, vb, *, blk, lam_init):
    b, t, _ = q.shape
    kern = functools.partial(_attn_prompt_kernel, blk=blk, lam_init=lam_init)
    seq = pl.BlockSpec((None, t, HEAD_W), lambda bi, h: (bi, 0, h))
    return pl.pallas_call(
        kern,
        grid=(b, N_HEADS),
        in_specs=[_resident((SUBLANES, LANES)), _resident((1, HEAD_W)), seq, seq, seq],
        out_specs=seq,
        out_shape=jax.ShapeDtypeStruct((b, t, D_ATTN), BF16),
        scratch_shapes=[pltpu.VMEM((t // blk, VT_ROWS, blk), BF16), pltpu.VMEM((t // blk, 2, HEAD_W, blk), BF16),
                        pltpu.VMEM((2, blk, blk), F32), pltpu.VMEM((2, blk, blk), F32),
                        pltpu.VMEM((2, 2, 1, blk), F32), pltpu.VMEM((2, 2, VT_ROWS, blk), F32)],
        compiler_params=_params("parallel", "parallel"),
        name="attn_prompt",
    )(lam, g_subln, q, kb, vb)


N_MAPROWS = 2 * N_HEADS
HALF_TOKENS = PAGE_SIZE // 2
HALF_KEYS = HALF_TOKENS * N_HEADS
QUARTER_TOKENS = PAGE_SIZE // 4
QUARTER_KEYS = QUARTER_TOKENS * N_HEADS


def _decode_weights(q_ref, kn_ref, k_refs):
    row = lax.broadcasted_iota(jnp.int32, (N_MAPROWS, HEAD_W), 0)
    lane = lax.broadcasted_iota(jnp.int32, (N_MAPROWS, HEAD_W), 1)
    q = q_ref[...].astype(F32)
    qm = jnp.where((lane >= HEAD_DIM) == (row >= N_HEADS), jnp.concatenate([q, q], axis=0), 0.0)
    zero = jnp.zeros_like(qm)
    qm2 = jnp.concatenate([jnp.concatenate([qm, zero], axis=1),
                           jnp.concatenate([zero, qm], axis=1)], axis=0).astype(BF16)

    krow = lax.broadcasted_iota(jnp.int32, (2 * N_MAPROWS, HALF_KEYS), 0)
    kcol = lax.broadcasted_iota(jnp.int32, (2 * N_MAPROWS, HALF_KEYS), 1)
    head_bits = N_HEADS - 1
    own = (kcol & head_bits) == (krow & head_bits)

    def scores(queries, keys):
        return lax.dot_general(queries, keys, (((1,), (1,)), ((), ())), preferred_element_type=F32)

    s_pages = [jnp.where(own, scores(qm2, _page_halves(kr)), NEG_INF) for kr in k_refs]
    s_new = jnp.where(lane == (row & head_bits), scores(qm.astype(BF16), _pad_new(kn_ref)), NEG_INF)

    m2 = jnp.max(s_pages[0], axis=-1, keepdims=True)
    for s in s_pages[1:]:
        m2 = jnp.maximum(m2, jnp.max(s, axis=-1, keepdims=True))
    m = jnp.maximum(jnp.maximum(m2[0:N_MAPROWS], m2[N_MAPROWS:2 * N_MAPROWS]), jnp.max(s_new, axis=-1, keepdims=True))
    mm = jnp.concatenate([m, m], axis=0)
    p_new = jnp.exp2(s_new - m)
    l = jnp.sum(p_new, axis=-1, keepdims=True)
    l2 = jnp.zeros((2 * N_MAPROWS, 1), F32)
    p_pages = []
    for s in s_pages:
        p = jnp.exp2(s - mm)
        l2 = l2 + jnp.sum(p, axis=-1, keepdims=True)
        p_pages.append(p.astype(BF16))
    l = l + l2[0:N_MAPROWS] + l2[N_MAPROWS:2 * N_MAPROWS]
    return p_pages, p_new.astype(BF16), l


def _page_halves(ref):
    a = ref[0:HALF_TOKENS].reshape(HALF_KEYS, HEAD_W)
    b = ref[HALF_TOKENS:PAGE_SIZE].reshape(HALF_KEYS, HEAD_W)
    return jnp.concatenate([a, b], axis=1).astype(BF16)


def _pad_new(ref):
    pad = jnp.zeros((LANES - N_HEADS, HEAD_W), F32)
    return jnp.concatenate([ref[...], pad], axis=0).astype(BF16)


def _page_quarters(ref):
    parts = [ref[i * QUARTER_TOKENS:(i + 1) * QUARTER_TOKENS].reshape(QUARTER_KEYS, HEAD_W) for i in range(4)]
    return jnp.concatenate(parts, axis=1).astype(BF16)


def _decode_output(weights, lam_ref, g_ref, vn_ref, v_refs, o_ref, *, lam_init):
    p_pages, p_new, l = weights
    acc = jnp.dot(p_new, _pad_new(vn_ref), preferred_element_type=F32)
    acc4 = jnp.zeros((4 * N_MAPROWS, 4 * HEAD_W), F32)
    for p, vr in zip(p_pages, v_refs):
        p4 = jnp.concatenate([p[0:N_MAPROWS, 0:QUARTER_KEYS], p[0:N_MAPROWS, QUARTER_KEYS:HALF_KEYS],
                              p[N_MAPROWS:2 * N_MAPROWS, 0:QUARTER_KEYS],
                              p[N_MAPROWS:2 * N_MAPROWS, QUARTER_KEYS:HALF_KEYS]], axis=0)
        acc4 = acc4 + jnp.dot(p4, _page_quarters(vr), preferred_element_type=F32)
    for i in range(4):
        acc = acc + acc4[i * N_MAPROWS:(i + 1) * N_MAPROWS, i * HEAD_W:(i + 1) * HEAD_W]
    a = acc / l
    o = a[0:N_HEADS] - lam_ref[0:1, 0:1] * a[N_HEADS:N_MAPROWS]
    o_ref[...] = _subln(o, g_ref[...], lam_init).astype(o_ref.dtype)


def _mix_kernel(x_ref, pooled_ref, on_ref, gate_ref, wg_ref, scale_ref, wpp_ref, wap_ref, wo_ref, o_ref):
    parts = []
    for g in range(len(POOL_WINDOWS)):
        sl = slice(g * POOL_GROUP, (g + 1) * POOL_GROUP)
        parts.append(jnp.dot(pooled_ref[:, sl], wg_ref[g], preferred_element_type=F32))
    pool_out = jnp.concatenate(parts, axis=1) * scale_ref[...]
    pool_proj = jnp.dot(pool_out.astype(BF16), wpp_ref[...], preferred_element_type=F32)
    attn_proj = jnp.dot(on_ref[...].astype(BF16), wap_ref[...], preferred_element_type=F32)
    merged = gate_ref[:, 0:D_MODEL] * pool_proj + gate_ref[:, D_MODEL:2 * D_MODEL] * attn_proj
    o_ref[...] = x_ref[...] + jnp.dot(merged.astype(BF16), wo_ref[...], preferred_element_type=F32)


def _mix(x2d, pooled, on, gates, wg_b, pool_scale, wpp_b, wap_b, wo_b, *, tm):
    m = x2d.shape[0]
    row = lambda i: (i, 0)
    return pl.pallas_call(
        _mix_kernel,
        grid=(m // tm,),
        in_specs=[pl.BlockSpec((tm, D_MODEL), row), pl.BlockSpec((tm, D_POOL), row),
                  pl.BlockSpec((tm, D_ATTN), row), pl.BlockSpec((tm, 2 * D_MODEL), row),
                  _resident(wg_b.shape), _resident((1, D_POOL)), _resident(wpp_b.shape),
                  _resident(wap_b.shape), _resident(wo_b.shape)],
        out_specs=pl.BlockSpec((tm, D_MODEL), row),
        out_shape=jax.ShapeDtypeStruct((m, D_MODEL), F32),
        compiler_params=_params("parallel"),
        name="mix",
    )(x2d, pooled, on, gates, wg_b, pool_scale, wpp_b, wap_b, wo_b)


MLP_CHUNK = 1024


def _rms(x, g):
    ms = jnp.mean(x * x, axis=-1, keepdims=True)
    return x * lax.rsqrt(ms + RMS_EPS) * g


def _mlp_kernel(x_ref, g_ref, wup_ref, wdn_ref, gf_ref, o_ref, h_scr):
    x = x_ref[...]
    h_scr[...] = _rms(x, g_ref[...]).astype(BF16)
    y = x
    for c in range(D_FF // MLP_CHUNK):
        sl = slice(c * MLP_CHUNK, (c + 1) * MLP_CHUNK)
        a = jnp.maximum(jnp.dot(h_scr[...], wup_ref[:, sl], preferred_element_type=F32), 0.0)
        y = y + jnp.dot((a * a).astype(BF16), wdn_ref[sl, :], preferred_element_type=F32)
    o_ref[...] = _rms(y, gf_ref[...])


def _mlp(x2d, g_mlp, wup_b, wdn_b, g_final, *, tm):
    m = x2d.shape[0]
    row = lambda i: (i, 0)
    return pl.pallas_call(
        _mlp_kernel,
        grid=(m // tm,),
        in_specs=[pl.BlockSpec((tm, D_MODEL), row), _resident((1, D_MODEL)), _resident(wup_b.shape),
                  _resident(wdn_b.shape), _resident((1, D_MODEL))],
        out_specs=pl.BlockSpec((tm, D_MODEL), row),
        out_shape=jax.ShapeDtypeStruct((m, D_MODEL), F32),
        scratch_shapes=[pltpu.VMEM((tm, D_MODEL), BF16)],
        compiler_params=_params("parallel"),
        name="mlp",
    )(x2d, g_mlp, wup_b, wdn_b, g_final)


N_MLP_IN = 5
N_DEC_IN = 5


def _mlp_decode_kernel(pt_ref, *refs, n_pages, lam_init):
    del pt_ref
    mlp_in = refs[:N_MLP_IN]
    lam_ref, gs_ref, q_ref, kn_ref, vn_ref = refs[N_MLP_IN:N_MLP_IN + N_DEC_IN]
    pages = refs[N_MLP_IN + N_DEC_IN:N_MLP_IN + N_DEC_IN + 2 * n_pages]
    y_ref, on_ref, h_scr = refs[N_MLP_IN + N_DEC_IN + 2 * n_pages:]
    weights = _decode_weights(q_ref, kn_ref, pages[:n_pages])
    _mlp_kernel(*mlp_in, y_ref, h_scr)
    _decode_output(weights, lam_ref, gs_ref, vn_ref, pages[n_pages:], on_ref, lam_init=lam_init)


def _mlp_decode(x2d, g_mlp, wup_b, wdn_b, g_final, page_table, lam, g_subln, q3, kn3, vn3, cache_k4, cache_v4,
                *, lam_init):
    m = x2d.shape[0]
    n, n_pages = page_table.shape
    tm = m // n
    assert tm * n == m and tm % SUBLANES == 0
    pt_flat = page_table.reshape(-1)
    const2 = lambda i, pt: (0, 0)
    row = pl.BlockSpec((tm, D_MODEL), lambda i, pt: (i, 0))
    seq = pl.BlockSpec((None, N_HEADS, HEAD_W), lambda i, pt: (i, 0, 0))

    def whole(shape):
        return pl.BlockSpec(shape, const2, pipeline_mode=pl.Buffered(1))

    def page_spec(j):
        return pl.BlockSpec((None, PAGE_SIZE, N_HEADS, HEAD_W), lambda i, pt: (pt[i * n_pages + j], 0, 0, 0))

    grid_spec = pltpu.PrefetchScalarGridSpec(
        num_scalar_prefetch=1,
        grid=(n,),
        in_specs=[row, whole((1, D_MODEL)), whole(wup_b.shape), whole(wdn_b.shape), whole((1, D_MODEL)),
                  whole((SUBLANES, LANES)), whole((1, HEAD_W)), seq, seq, seq]
                 + [page_spec(j) for j in range(n_pages)] * 2,
        out_specs=(row, seq),
        scratch_shapes=[pltpu.VMEM((tm, D_MODEL), BF16)],
    )
    kern = functools.partial(_mlp_decode_kernel, n_pages=n_pages, lam_init=lam_init)
    return pl.pallas_call(
        kern,
        grid_spec=grid_spec,
        out_shape=(jax.ShapeDtypeStruct((m, D_MODEL), F32), jax.ShapeDtypeStruct((n, N_HEADS, HEAD_W), F32)),
        compiler_params=_params("parallel"),
        name="mlp_decode",
    )(pt_flat, x2d, g_mlp, wup_b, wdn_b, g_final, lam, g_subln, q3, kn3, vn3,
      *([cache_k4] * n_pages), *([cache_v4] * n_pages))


PROMPT_TM = 512
ATTN_BLK = 512


def kernel(x_prompt, x_sample, state_pool, cache_k, cache_v, page_table, g_mix, w_in, w_pool_grp, pool_scale,
           w_pool_proj, lambda_q1, lambda_k1, lambda_q2, lambda_k2, g_subln, w_attn_proj, w_o, g_mlp, w_up,
           w_down, g_final):
    depth = w_in.shape[0]
    assert depth == 1, "single-layer trunk"
    layer = 0
    lam_init = _lambda_init(layer)
    b, t, _ = x_prompt.shape
    n, t_s, _ = x_sample.shape
    assert t_s == 1 and t % PROMPT_TM == 0 and t % ATTN_BLK == 0
    past_len = page_table.shape[1] * PAGE_SIZE

    w_in_b = w_in[layer].astype(BF16)
    wg_b = w_pool_grp[layer].astype(BF16)
    wpp_b = w_pool_proj[layer].astype(BF16)
    wap_b = w_attn_proj[layer].astype(BF16)
    wo_b = w_o[layer].astype(BF16)
    wup_b = w_up[layer].astype(BF16)
    wdn_b = w_down[layer].astype(BF16)
    g_mix2 = g_mix[layer][None, :]
    g_mlp2 = g_mlp[layer][None, :]
    g_fin2 = g_final[None, :]
    g_sub2 = g_subln[layer][None, :]
    scale2 = pool_scale[layer][None, :]

    cos_tab, sin_tab, lam = _prep(lambda_q1[layer], lambda_k1[layer], lambda_q2[layer], lambda_k2[layer],
                                  n_prompt=t, n_sample_rows=n, past_len=past_len, lam_init=lam_init)

    tiles_per_seq = t // PROMPT_TM
    xp2 = x_prompt.reshape(b * t, D_MODEL)
    u_p, q_p, k_p, v_p, kb_p, vb_p, gates_p, pooled_p = _proj(
        xp2, g_mix2, w_in_b, cos_tab, sin_tab, tm=PROMPT_TM, tab_block_of=lambda i: i % tiles_per_seq,
        tiles_per_seq=tiles_per_seq)
    xs2 = x_sample.reshape(n, D_MODEL)
    u_s, q_s, k_s, v_s, _, _, gates_s = _proj(
        xs2, g_mix2, w_in_b, cos_tab, sin_tab, tm=n, tab_block_of=lambda i: t // n)

    on_p = _attn_prompt(lam, g_sub2, q_p.reshape(b, t, D_ATTN), kb_p.reshape(b, t, D_ATTN),
                        vb_p.reshape(b, t, D_ATTN), blk=ATTN_BLK, lam_init=lam_init)
    x1_p = _mix(xp2, pooled_p, on_p.reshape(b * t, D_ATTN), gates_p, wg_b, scale2, wpp_b, wap_b, wo_b,
                tm=PROMPT_TM)

    heads = lambda a: a.reshape(n, N_HEADS, HEAD_W)
    y_p, on_s = _mlp_decode(x1_p, g_mlp2, wup_b, wdn_b, g_fin2, page_table, lam, g_sub2,
                            heads(q_s), heads(k_s), heads(v_s), cache_k[layer], cache_v[layer], lam_init=lam_init)

    state_t = jnp.swapaxes(state_pool[layer], 0, 1)
    pooled_s = _pool_sample(state_t, u_s, past_len=past_len)
    x1_s = _mix(xs2, pooled_s, on_s.reshape(n, D_ATTN), gates_s, wg_b, scale2, wpp_b, wap_b, wo_b, tm=n)
    y_s = _mlp(x1_s, g_mlp2, wup_b, wdn_b, g_fin2, tm=n)

    y_prompt = y_p.reshape(b, t, D_MODEL)
    y_sample = y_s.reshape(n, 1, D_MODEL)
    new_k_prompt = k_p.reshape(1, b, t, N_HEADS, HEAD_W)
    new_v_prompt = v_p.reshape(1, b, t, N_HEADS, HEAD_W)
    new_pool_prompt = u_p.reshape(b, t, D_POOL)[:, t - POOL_BUF:, :][None]
    new_k_sample = k_s.reshape(1, n, 1, N_HEADS, HEAD_W)
    new_v_sample = v_s.reshape(1, n, 1, N_HEADS, HEAD_W)
    new_pool_sample = jnp.concatenate([state_pool[layer][:, 1:, :], u_s[:, None, :]], axis=1)[None]
    return (y_prompt, y_sample, new_k_prompt, new_v_prompt, new_pool_prompt,
            new_k_sample, new_v_sample, new_pool_sample)
```

```python
import functools
import math

import jax
import jax.numpy as jnp
from jax import lax
from jax.experimental import pallas as pl
from jax.experimental.pallas import tpu as pltpu

D_MODEL = 1024
N_HEADS = 8
HEAD_DIM = 64
HEAD_W = 2 * HEAD_DIM
D_ATTN = N_HEADS * HEAD_W
POOL_WINDOWS = (2, 4, 8, 16)
D_POOL = D_MODEL // 2
POOL_GROUP = D_POOL // len(POOL_WINDOWS)
POOL_BUF = max(POOL_WINDOWS) - 1
HALO = 16
D_FF = 4 * D_MODEL
ROPE_THETA = 10000.0
RMS_EPS = 1e-6
SUBLN_EPS = 1e-5
NEG_INF = -1e30
PAGE_SIZE = 128
OFF_Q = D_POOL
OFF_K = OFF_Q + D_ATTN
OFF_V = OFF_K + D_ATTN
OFF_G = OFF_V + D_ATTN
D_IN = OFF_G + 2 * D_MODEL

LANES = 128
SUBLANES = 8
VMEM_LIMIT = 56 * 1024 * 1024

BF16 = jnp.bfloat16
F32 = jnp.float32


def _lambda_init(layer):
    return 0.8 - 0.6 * math.exp(-0.3 * layer)


def _resident(shape):
    nd = len(shape)
    return pl.BlockSpec(shape, lambda *_: (0,) * nd, pipeline_mode=pl.Buffered(1))


def _slab(block, axis, c):
    index = (lambda *_: (c, 0)) if axis == 0 else (lambda *_: (0, c))
    return pl.BlockSpec(block, index, pipeline_mode=pl.Buffered(1))


def _params(*sem):
    return pltpu.CompilerParams(dimension_semantics=sem, vmem_limit_bytes=VMEM_LIMIT)


def _prep_kernel(invf_ref, lq1_ref, lk1_ref, lq2_ref, lk2_ref, cos_ref, sin_ref, lam_ref, *, n_prompt, past_len, lam_init):
    rows = cos_ref.shape[0]
    row = lax.broadcasted_iota(jnp.int32, (rows, LANES), 0)
    lane = lax.broadcasted_iota(jnp.int32, (rows, LANES), 1)
    pos = jnp.where(row < n_prompt, row, past_len).astype(F32)
    ang = pos * invf_ref[...]
    cos_ref[...] = jnp.cos(ang)
    sin_ref[...] = jnp.where((lane % HEAD_DIM) < HEAD_DIM // 2, -jnp.sin(ang), jnp.sin(ang))
    d1 = jnp.sum(lq1_ref[...] * lk1_ref[...], axis=-1, keepdims=True)
    d2 = jnp.sum(lq2_ref[...] * lk2_ref[...], axis=-1, keepdims=True)
    lam = jnp.exp(d1) - jnp.exp(d2) + lam_init
    lam_ref[...] = jnp.broadcast_to(lam, lam_ref.shape)


def _prep(lq1, lk1, lq2, lk2, *, n_prompt, n_sample_rows, past_len, lam_init):
    half = HEAD_DIM // 2
    inv_freq = ROPE_THETA ** (-jnp.arange(half, dtype=F32) * (2.0 / HEAD_DIM))
    invf = jnp.tile(inv_freq, LANES // half)[None, :]
    rows = n_prompt + n_sample_rows
    kern = functools.partial(_prep_kernel, n_prompt=n_prompt, past_len=past_len, lam_init=lam_init)
    return pl.pallas_call(
        kern,
        out_shape=(jax.ShapeDtypeStruct((rows, LANES), F32),
                   jax.ShapeDtypeStruct((rows, LANES), F32),
                   jax.ShapeDtypeStruct((SUBLANES, LANES), F32)),
        name="prep",
    )(invf, lq1[None, :], lk1[None, :], lq2[None, :], lk2[None, :])


PROJ_CHUNK = 512
Q_SCALE = HEAD_DIM ** -0.5 * math.log2(math.e)
N_W_CHUNKS = D_IN // PROJ_CHUNK


def _rope(z, cos, sin, first_half):
    swapped = jnp.where(first_half, pltpu.roll(z, LANES - HEAD_DIM // 2, 1), pltpu.roll(z, HEAD_DIM // 2, 1))
    return z * cos + swapped * sin


POOL_ROWS = 64


def _window_means(ext_scr, pos, o_ref, tm):
    for g, w in enumerate(POOL_WINDOWS):
        sl = slice(g * POOL_GROUP, (g + 1) * POOL_GROUP)
        for r0 in range(0, tm, POOL_ROWS):
            cur = ext_scr[HALO + r0:HALO + r0 + POOL_ROWS, sl]
            acc = cur
            for k in range(1, w):
                acc = acc + ext_scr[HALO + r0 - k:HALO + r0 - k + POOL_ROWS, sl]
            count = jnp.minimum(pos[r0:r0 + POOL_ROWS] + 1, w).astype(F32)
            o_ref[r0:r0 + POOL_ROWS, sl] = (acc / count - cur).astype(o_ref.dtype)


def _proj_kernel(x_ref, g_ref, *refs, tiles_per_seq):
    w_refs = refs[:N_W_CHUNKS]
    cos_ref, sin_ref, u_ref, q_ref, k_ref, v_ref, kb_ref, vb_ref, gate_ref = refs[N_W_CHUNKS:N_W_CHUNKS + 9]
    rest = refs[N_W_CHUNKS + 9:]
    if tiles_per_seq is None:
        (h_scr,) = rest
    else:
        pooled_ref, h_scr, ext_scr = rest
    tm = x_ref.shape[0]
    if tiles_per_seq is not None:
        i = pl.program_id(0) % tiles_per_seq

        @pl.when(i == 0)
        def _():
            ext_scr[0:HALO, :] = jnp.zeros((HALO, D_POOL), F32)

        @pl.when(i > 0)
        def _():
            ext_scr[0:HALO, :] = ext_scr[tm:tm + HALO, :]

    x = x_ref[...]
    ms = jnp.mean(x * x, axis=-1, keepdims=True)
    h_scr[...] = (x * lax.rsqrt(ms + RMS_EPS) * g_ref[...]).astype(BF16)
    cos = cos_ref[...]
    sin = sin_ref[...]
    lane = lax.broadcasted_iota(jnp.int32, (tm, LANES), 1)
    first_half = (lane % HEAD_DIM) < HEAD_DIM // 2

    def dot(off, width):
        assert off % PROJ_CHUNK == 0 and width == PROJ_CHUNK
        return jnp.dot(h_scr[...], w_refs[off // PROJ_CHUNK][...], preferred_element_type=F32)

    u = dot(0, D_POOL)
    u_ref[...] = u
    if tiles_per_seq is not None:
        ext_scr[HALO:HALO + tm, :] = u
    for c in range(D_ATTN // PROJ_CHUNK):
        zq = dot(OFF_Q + c * PROJ_CHUNK, PROJ_CHUNK)
        zk = dot(OFF_K + c * PROJ_CHUNK, PROJ_CHUNK)
        for j in range(PROJ_CHUNK // LANES):
            sl = slice(j * LANES, (j + 1) * LANES)
            osl = slice(c * PROJ_CHUNK + j * LANES, c * PROJ_CHUNK + (j + 1) * LANES)
            q_ref[:, osl] = (_rope(zq[:, sl], cos, sin, first_half) * Q_SCALE).astype(BF16)
            kr = _rope(zk[:, sl], cos, sin, first_half)
            k_ref[:, osl] = kr
            kb_ref[:, osl] = kr.astype(BF16)
        zv = dot(OFF_V + c * PROJ_CHUNK, PROJ_CHUNK)
        csl = slice(c * PROJ_CHUNK, (c + 1) * PROJ_CHUNK)
        v_ref[:, csl] = zv
        vb_ref[:, csl] = zv.astype(BF16)
    if tiles_per_seq is not None:
        _window_means(ext_scr, i * tm + lax.broadcasted_iota(jnp.int32, (tm, POOL_GROUP), 0), pooled_ref, tm)
    for c in range(2 * D_MODEL // PROJ_CHUNK):
        zg = dot(OFF_G + c * PROJ_CHUNK, PROJ_CHUNK)
        gate_ref[:, c * PROJ_CHUNK:(c + 1) * PROJ_CHUNK] = jax.nn.sigmoid(zg)


def _proj(x2d, g_mix, w_in_b, cos_tab, sin_tab, *, tm, tab_block_of, tiles_per_seq=None):
    m = x2d.shape[0]
    row = lambda i: (i, 0)
    tab = pl.BlockSpec((tm, LANES), lambda i: (tab_block_of(i), 0))
    outs = [
        jax.ShapeDtypeStruct((m, D_POOL), F32),
        jax.ShapeDtypeStruct((m, D_ATTN), BF16),
        jax.ShapeDtypeStruct((m, D_ATTN), F32),
        jax.ShapeDtypeStruct((m, D_ATTN), F32),
        jax.ShapeDtypeStruct((m, D_ATTN), BF16),
        jax.ShapeDtypeStruct((m, D_ATTN), BF16),
        jax.ShapeDtypeStruct((m, 2 * D_MODEL), F32),
    ]
    scratch = [pltpu.VMEM((tm, D_MODEL), BF16)]
    if tiles_per_seq is not None:
        outs.append(jax.ShapeDtypeStruct((m, D_POOL), BF16))
        scratch.append(pltpu.VMEM((HALO + tm, D_POOL), F32))
    return pl.pallas_call(
        functools.partial(_proj_kernel, tiles_per_seq=tiles_per_seq),
        grid=(m // tm,),
        in_specs=[pl.BlockSpec((tm, D_MODEL), row), _resident((1, D_MODEL))]
                 + [_slab((D_MODEL, PROJ_CHUNK), 1, c) for c in range(N_W_CHUNKS)] + [tab, tab],
        out_specs=tuple(pl.BlockSpec((tm, s.shape[1]), row) for s in outs),
        out_shape=tuple(outs),
        scratch_shapes=scratch,
        compiler_params=_params("parallel" if tiles_per_seq is None else "arbitrary"),
        name="proj",
    )(x2d, g_mix, *([w_in_b] * N_W_CHUNKS), cos_tab, sin_tab)


def _pool_sample_kernel(state_ref, u_ref, o_ref, *, past_len):
    for g, w in enumerate(POOL_WINDOWS):
        sl = slice(g * POOL_GROUP, (g + 1) * POOL_GROUP)
        cur = u_ref[:, sl]
        acc = cur
        for k in range(1, w):
            acc = acc + state_ref[POOL_BUF - k, :, sl]
        count = float(min(past_len + 1, w))
        o_ref[:, sl] = (acc / count - cur).astype(o_ref.dtype)


def _pool_sample(state_t, u_s, *, past_len):
    n = u_s.shape[0]
    return pl.pallas_call(
        functools.partial(_pool_sample_kernel, past_len=past_len),
        out_shape=jax.ShapeDtypeStruct((n, D_POOL), BF16),
        name="pool_sample",
    )(state_t, u_s)


def _subln(o, g, lam_init):
    ms = jnp.mean(o * o, axis=-1, keepdims=True)
    return o * lax.rsqrt(ms + SUBLN_EPS) * g * (1.0 - lam_init)


ONES_ROWS = 16
VT_ROWS = HEAD_W + ONES_ROWS
COL_STRIP = 256


def _attn_prompt_kernel(lam_ref, g_ref, q_ref, k_ref, v_ref, o_ref,
                        vt_scr, qt_scr, s0_scr, s1_scr, m_scr, acc_scr, *, blk, lam_init):
    nblk = k_ref.shape[0] // blk
    drow = lax.broadcasted_iota(jnp.int32, (HEAD_W, blk), 0)
    for r in range(nblk):
        rows = slice(r * blk, (r + 1) * blk)
        vt_scr[r, 0:HEAD_W, :] = v_ref[rows, :].astype(F32).T.astype(BF16)
        vt_scr[r, HEAD_W:VT_ROWS, :] = jnp.ones((ONES_ROWS, blk), BF16)
        qt = q_ref[rows, :].astype(F32).T
        qt_scr[r, 0] = jnp.where(drow < HEAD_DIM, qt, 0.0).astype(BF16)
        qt_scr[r, 1] = jnp.where(drow >= HEAD_DIM, qt, 0.0).astype(BF16)
    krow = lax.broadcasted_iota(jnp.int32, (blk, blk), 0)
    qcol = lax.broadcasted_iota(jnp.int32, (blk, blk), 1)
    causal = krow <= qcol
    lam = lam_ref[0:1, 0:1]
    gain = g_ref[...] * (1.0 - lam_init)
    s_bufs = (s0_scr, s1_scr)
    units = [(qi, j) for qi in range(nblk) for j in range(qi + 1)]

    def fill(u):
        qi, j = units[u]
        k = k_ref[j * blk:(j + 1) * blk, :]
        for c in range(2):
            s = jnp.dot(k, qt_scr[qi, c], preferred_element_type=F32)
            s_bufs[u % 2][c] = jnp.where(causal, s, NEG_INF) if j == qi else s

    def consume(u):
        qi, j = units[u]
        st = qi % 2
        vt = vt_scr[j]
        for c in range(2):
            for h in range(blk // COL_STRIP):
                cols = slice(h * COL_STRIP, (h + 1) * COL_STRIP)
                s_ref = s_bufs[u % 2].at[c, :, cols]
                m_blk = jnp.max(s_ref[...], axis=0, keepdims=True)
                if j == 0:
                    m_new = m_blk
                    p = jnp.exp2(s_ref[...] - m_new).astype(BF16)
                    acc_scr[st, c, :, cols] = jnp.dot(vt, p, preferred_element_type=F32)
                else:
                    m_prev = m_scr[st, c, :, cols]
                    m_new = jnp.maximum(m_prev, m_blk)
                    p = jnp.exp2(s_ref[...] - m_new).astype(BF16)
                    acc_scr[st, c, :, cols] = (jnp.exp2(m_prev - m_new) * acc_scr[st, c, :, cols]
                                               + jnp.dot(vt, p, preferred_element_type=F32))
                m_scr[st, c, :, cols] = m_new

    def finalize(qi):
        a1 = acc_scr[qi % 2, 0]
        a2 = acc_scr[qi % 2, 1]
        ot = (a1[0:HEAD_W] * (1.0 / a1[HEAD_W:HEAD_W + 1])
              - lam * (a2[0:HEAD_W] * (1.0 / a2[HEAD_W:HEAD_W + 1])))
        ms = jnp.mean(ot * ot, axis=0, keepdims=True)
        on = (ot * lax.rsqrt(ms + SUBLN_EPS)).T * gain
        o_ref[qi * blk:(qi + 1) * blk, :] = on.astype(o_ref.dtype)

    fill(0)
    for u, (qi, j) in enumerate(units):
        if u + 1 < len(units):
            fill(u + 1)
        consume(u)
        if j == qi:
            finalize(qi)


def _attn_prompt(lam, g_subln, q, kb, vb, *, blk, lam_init):
    b, t, _ = q.shape
    kern = functools.partial(_attn_prompt_kernel, blk=blk, lam_init=lam_init)
    seq = pl.BlockSpec((None, t, HEAD_W), lambda bi, h: (bi, 0, h))
    return pl.pallas_call(
        kern,
        grid=(b, N_HEADS),
        in_specs=[_resident((SUBLANES, LANES)), _resident((1, HEAD_W)), seq, seq, seq],
        out_specs=seq,
        out_shape=jax.ShapeDtypeStruct((b, t, D_ATTN), BF16),
        scratch_shapes=[pltpu.VMEM((t // blk, VT_ROWS, blk), BF16), pltpu.VMEM((t // blk, 2, HEAD_W, blk), BF16),
                        pltpu.VMEM((2, blk, blk), F32), pltpu.VMEM((2, blk, blk), F32),
                        pltpu.VMEM((2, 2, 1, blk), F32), pltpu.VMEM((2, 2, VT_ROWS, blk), F32)],
        compiler_params=_params("parallel", "parallel"),
        name="attn_prompt",
    )(lam, g_subln, q, kb, vb)


N_MAPROWS = 2 * N_HEADS
HALF_TOKENS = PAGE_SIZE // 2
HALF_KEYS = HALF_TOKENS * N_HEADS
QUARTER_TOKENS = PAGE_SIZE // 4
QUARTER_KEYS = QUARTER_TOKENS * N_HEADS


def _decode_weights(q_ref, kn_ref, k_refs):
    row = lax.broadcasted_iota(jnp.int32, (N_MAPROWS, HEAD_W), 0)
    lane = lax.broadcasted_iota(jnp.int32, (N_MAPROWS, HEAD_W), 1)
    q = q_ref[...].astype(F32)
    qm = jnp.where((lane >= HEAD_DIM) == (row >= N_HEADS), jnp.concatenate([q, q], axis=0), 0.0)
    zero = jnp.zeros_like(qm)
    qm2 = jnp.concatenate([jnp.concatenate([qm, zero], axis=1),
                           jnp.concatenate([zero, qm], axis=1)], axis=0).astype(BF16)

    krow = lax.broadcasted_iota(jnp.int32, (2 * N_MAPROWS, HALF_KEYS), 0)
    kcol = lax.broadcasted_iota(jnp.int32, (2 * N_MAPROWS, HALF_KEYS), 1)
    head_bits = N_HEADS - 1
    own = (kcol & head_bits) == (krow & head_bits)

    def scores(queries, keys):
        return lax.dot_general(queries, keys, (((1,), (1,)), ((), ())), preferred_element_type=F32)

    s_pages = [jnp.where(own, scores(qm2, _page_halves(kr)), NEG_INF) for kr in k_refs]
    s_new = jnp.where(lane == (row & head_bits), scores(qm.astype(BF16), _pad_new(kn_ref)), NEG_INF)

    m2 = jnp.max(s_pages[0], axis=-1, keepdims=True)
    for s in s_pages[1:]:
        m2 = jnp.maximum(m2, jnp.max(s, axis=-1, keepdims=True))
    m = jnp.maximum(jnp.maximum(m2[0:N_MAPROWS], m2[N_MAPROWS:2 * N_MAPROWS]), jnp.max(s_new, axis=-1, keepdims=True))
    mm = jnp.concatenate([m, m], axis=0)
    p_new = jnp.exp2(s_new - m)
    l = jnp.sum(p_new, axis=-1, keepdims=True)
    l2 = jnp.zeros((2 * N_MAPROWS, 1), F32)
    p_pages = []
    for s in s_pages:
        p = jnp.exp2(s - mm)
        l2 = l2 + jnp.sum(p, axis=-1, keepdims=True)
        p_pages.append(p.astype(BF16))
    l = l + l2[0:N_MAPROWS] + l2[N_MAPROWS:2 * N_MAPROWS]
    return p_pages, p_new.astype(BF16), l


def _page_halves(ref):
    a = ref[0:HALF_TOKENS].reshape(HALF_KEYS, HEAD_W)
    b = ref[HALF_TOKENS:PAGE_SIZE].reshape(HALF_KEYS, HEAD_W)
    return jnp.concatenate([a, b], axis=1).astype(BF16)


def _pad_new(ref):
    pad = jnp.zeros((LANES - N_HEADS, HEAD_W), F32)
    return jnp.concatenate([ref[...], pad], axis=0).astype(BF16)


def _page_quarters(ref):
    parts = [ref[i * QUARTER_TOKENS:(i + 1) * QUARTER_TOKENS].reshape(QUARTER_KEYS, HEAD_W) for i in range(4)]
    return jnp.concatenate(parts, axis=1).astype(BF16)


def _decode_output(weights, lam_ref, g_ref, vn_ref, v_refs, o_ref, *, lam_init):
    p_pages, p_new, l = weights
    acc = jnp.dot(p_new, _pad_new(vn_ref), preferred_element_type=F32)
    acc4 = jnp.zeros((4 * N_MAPROWS, 4 * HEAD_W), F32)
    for p, vr in zip(p_pages, v_refs):
        p4 = jnp.concatenate([p[0:N_MAPROWS, 0:QUARTER_KEYS], p[0:N_MAPROWS, QUARTER_KEYS:HALF_KEYS],
                              p[N_MAPROWS:2 * N_MAPROWS, 0:QUARTER_KEYS],
                              p[N_MAPROWS:2 * N_MAPROWS, QUARTER_KEYS:HALF_KEYS]], axis=0)
        acc4 = acc4 + jnp.dot(p4, _page_quarters(vr), preferred_element_type=F32)
    for i in range(4):
        acc = acc + acc4[i * N_MAPROWS:(i + 1) * N_MAPROWS, i * HEAD_W:(i + 1) * HEAD_W]
    a = acc / l
    o = a[0:N_HEADS] - lam_ref[0:1, 0:1] * a[N_HEADS:N_MAPROWS]
    o_ref[...] = _subln(o, g_ref[...], lam_init).astype(o_ref.dtype)


def _mix_kernel(x_ref, pooled_ref, on_ref, gate_ref, wg_ref, scale_ref, wpp_ref, wap_ref, wo_ref, o_ref):
    parts = []
    for g in range(len(POOL_WINDOWS)):
        sl = slice(g * POOL_GROUP, (g + 1) * POOL_GROUP)
        parts.append(jnp.dot(pooled_ref[:, sl], wg_ref[g], preferred_element_type=F32))
    pool_out = jnp.concatenate(parts, axis=1) * scale_ref[...]
    pool_proj = jnp.dot(pool_out.astype(BF16), wpp_ref[...], preferred_element_type=F32)
    attn_proj = jnp.dot(on_ref[...].astype(BF16), wap_ref[...], preferred_element_type=F32)
    merged = gate_ref[:, 0:D_MODEL] * pool_proj + gate_ref[:, D_MODEL:2 * D_MODEL] * attn_proj
    o_ref[...] = x_ref[...] + jnp.dot(merged.astype(BF16), wo_ref[...], preferred_element_type=F32)


def _mix(x2d, pooled, on, gates, wg_b, pool_scale, wpp_b, wap_b, wo_b, *, tm):
    m = x2d.shape[0]
    row = lambda i: (i, 0)
    return pl.pallas_call(
        _mix_kernel,
        grid=(m // tm,),
        in_specs=[pl.BlockSpec((tm, D_MODEL), row), pl.BlockSpec((tm, D_POOL), row),
                  pl.BlockSpec((tm, D_ATTN), row), pl.BlockSpec((tm, 2 * D_MODEL), row),
                  _resident(wg_b.shape), _resident((1, D_POOL)), _resident(wpp_b.shape),
                  _resident(wap_b.shape), _resident(wo_b.shape)],
        out_specs=pl.BlockSpec((tm, D_MODEL), row),
        out_shape=jax.ShapeDtypeStruct((m, D_MODEL), F32),
        compiler_params=_params("parallel"),
        name="mix",
    )(x2d, pooled, on, gates, wg_b, pool_scale, wpp_b, wap_b, wo_b)


MLP_CHUNK = 1024


def _rms(x, g):
    ms = jnp.mean(x * x, axis=-1, keepdims=True)
    return x * lax.rsqrt(ms + RMS_EPS) * g


N_FF_CHUNKS = D_FF // MLP_CHUNK


def _mlp_kernel(x_ref, g_ref, *refs):
    wup_refs = refs[:N_FF_CHUNKS]
    wdn_refs = refs[N_FF_CHUNKS:2 * N_FF_CHUNKS]
    gf_ref, o_ref, h_scr = refs[2 * N_FF_CHUNKS:]
    x = x_ref[...]
    h_scr[...] = _rms(x, g_ref[...]).astype(BF16)
    y = x
    for c in range(N_FF_CHUNKS):
        a = jnp.maximum(jnp.dot(h_scr[...], wup_refs[c][...], preferred_element_type=F32), 0.0)
        y = y + jnp.dot((a * a).astype(BF16), wdn_refs[c][...], preferred_element_type=F32)
    o_ref[...] = _rms(y, gf_ref[...])


def _mlp_weight_specs():
    return ([_slab((D_MODEL, MLP_CHUNK), 1, c) for c in range(N_FF_CHUNKS)]
            + [_slab((MLP_CHUNK, D_MODEL), 0, c) for c in range(N_FF_CHUNKS)])


def _mlp(x2d, g_mlp, wup_b, wdn_b, g_final, *, tm):
    m = x2d.shape[0]
    row = lambda i: (i, 0)
    return pl.pallas_call(
        _mlp_kernel,
        grid=(m // tm,),
        in_specs=[pl.BlockSpec((tm, D_MODEL), row), _resident((1, D_MODEL))] + _mlp_weight_specs()
                 + [_resident((1, D_MODEL))],
        out_specs=pl.BlockSpec((tm, D_MODEL), row),
        out_shape=jax.ShapeDtypeStruct((m, D_MODEL), F32),
        scratch_shapes=[pltpu.VMEM((tm, D_MODEL), BF16)],
        compiler_params=_params("parallel"),
        name="mlp",
    )(x2d, g_mlp, *([wup_b] * N_FF_CHUNKS), *([wdn_b] * N_FF_CHUNKS), g_final)


N_MLP_IN = 3 + 2 * N_FF_CHUNKS
N_DEC_IN = 5


def _mlp_decode_kernel(pt_ref, *refs, n_pages, lam_init):
    del pt_ref
    mlp_in = refs[:N_MLP_IN]
    lam_ref, gs_ref, q_ref, kn_ref, vn_ref = refs[N_MLP_IN:N_MLP_IN + N_DEC_IN]
    pages = refs[N_MLP_IN + N_DEC_IN:N_MLP_IN + N_DEC_IN + 2 * n_pages]
    y_ref, on_ref, h_scr = refs[N_MLP_IN + N_DEC_IN + 2 * n_pages:]
    weights = _decode_weights(q_ref, kn_ref, pages[:n_pages])
    _mlp_kernel(*mlp_in, y_ref, h_scr)
    _decode_output(weights, lam_ref, gs_ref, vn_ref, pages[n_pages:], on_ref, lam_init=lam_init)


def _mlp_decode(x2d, g_mlp, wup_b, wdn_b, g_final, page_table, lam, g_subln, q3, kn3, vn3, cache_k4, cache_v4,
                *, lam_init):
    m = x2d.shape[0]
    n, n_pages = page_table.shape
    tm = m // n
    assert tm * n == m and tm % SUBLANES == 0
    pt_flat = page_table.reshape(-1)
    const2 = lambda i, pt: (0, 0)
    row = pl.BlockSpec((tm, D_MODEL), lambda i, pt: (i, 0))
    seq = pl.BlockSpec((None, N_HEADS, HEAD_W), lambda i, pt: (i, 0, 0))

    def whole(shape):
        return pl.BlockSpec(shape, const2, pipeline_mode=pl.Buffered(1))

    def page_spec(j):
        return pl.BlockSpec((None, PAGE_SIZE, N_HEADS, HEAD_W), lambda i, pt: (pt[i * n_pages + j], 0, 0, 0))

    grid_spec = pltpu.PrefetchScalarGridSpec(
        num_scalar_prefetch=1,
        grid=(n,),
        in_specs=[row, whole((1, D_MODEL))] + _mlp_weight_specs()
                 + [whole((1, D_MODEL)), whole((SUBLANES, LANES)), whole((1, HEAD_W)), seq, seq, seq]
                 + [page_spec(j) for j in range(n_pages)] * 2,
        out_specs=(row, seq),
        scratch_shapes=[pltpu.VMEM((tm, D_MODEL), BF16)],
    )
    kern = functools.partial(_mlp_decode_kernel, n_pages=n_pages, lam_init=lam_init)
    return pl.pallas_call(
        kern,
        grid_spec=grid_spec,
        out_shape=(jax.ShapeDtypeStruct((m, D_MODEL), F32), jax.ShapeDtypeStruct((n, N_HEADS, HEAD_W), F32)),
        compiler_params=_params("parallel"),
        name="mlp_decode",
    )(pt_flat, x2d, g_mlp, *([wup_b] * N_FF_CHUNKS), *([wdn_b] * N_FF_CHUNKS), g_final, lam, g_subln, q3, kn3, vn3,
      *([cache_k4] * n_pages), *([cache_v4] * n_pages))


PROMPT_TM = 512
ATTN_BLK = 512


def kernel(x_prompt, x_sample, state_pool, cache_k, cache_v, page_table, g_mix, w_in, w_pool_grp, pool_scale,
           w_pool_proj, lambda_q1, lambda_k1, lambda_q2, lambda_k2, g_subln, w_attn_proj, w_o, g_mlp, w_up,
           w_down, g_final):
    depth = w_in.shape[0]
    assert depth == 1, "single-layer trunk"
    layer = 0
    lam_init = _lambda_init(layer)
    b, t, _ = x_prompt.shape
    n, t_s, _ = x_sample.shape
    assert t_s == 1 and t % PROMPT_TM == 0 and t % ATTN_BLK == 0
    past_len = page_table.shape[1] * PAGE_SIZE

    w_in_b = w_in[layer].astype(BF16)
    wg_b = w_pool_grp[layer].astype(BF16)
    wpp_b = w_pool_proj[layer].astype(BF16)
    wap_b = w_attn_proj[layer].astype(BF16)
    wo_b = w_o[layer].astype(BF16)
    wup_b = w_up[layer].astype(BF16)
    wdn_b = w_down[layer].astype(BF16)
    g_mix2 = g_mix[layer][None, :]
    g_mlp2 = g_mlp[layer][None, :]
    g_fin2 = g_final[None, :]
    g_sub2 = g_subln[layer][None, :]
    scale2 = pool_scale[layer][None, :]

    cos_tab, sin_tab, lam = _prep(lambda_q1[layer], lambda_k1[layer], lambda_q2[layer], lambda_k2[layer],
                                  n_prompt=t, n_sample_rows=n, past_len=past_len, lam_init=lam_init)

    tiles_per_seq = t // PROMPT_TM
    xp2 = x_prompt.reshape(b * t, D_MODEL)
    u_p, q_p, k_p, v_p, kb_p, vb_p, gates_p, pooled_p = _proj(
        xp2, g_mix2, w_in_b, cos_tab, sin_tab, tm=PROMPT_TM, tab_block_of=lambda i: i % tiles_per_seq,
        tiles_per_seq=tiles_per_seq)
    xs2 = x_sample.reshape(n, D_MODEL)
    u_s, q_s, k_s, v_s, _, _, gates_s = _proj(
        xs2, g_mix2, w_in_b, cos_tab, sin_tab, tm=n, tab_block_of=lambda i: t // n)

    on_p = _attn_prompt(lam, g_sub2, q_p.reshape(b, t, D_ATTN), kb_p.reshape(b, t, D_ATTN),
                        vb_p.reshape(b, t, D_ATTN), blk=ATTN_BLK, lam_init=lam_init)
    x1_p = _mix(xp2, pooled_p, on_p.reshape(b * t, D_ATTN), gates_p, wg_b, scale2, wpp_b, wap_b, wo_b,
                tm=PROMPT_TM)

    heads = lambda a: a.reshape(n, N_HEADS, HEAD_W)
    y_p, on_s = _mlp_decode(x1_p, g_mlp2, wup_b, wdn_b, g_fin2, page_table, lam, g_sub2,
                            heads(q_s), heads(k_s), heads(v_s), cache_k[layer], cache_v[layer], lam_init=lam_init)

    state_t = jnp.swapaxes(state_pool[layer], 0, 1)
    pooled_s = _pool_sample(state_t, u_s, past_len=past_len)
    x1_s = _mix(xs2, pooled_s, on_s.reshape(n, D_ATTN), gates_s, wg_b, scale2, wpp_b, wap_b, wo_b, tm=n)
    y_s = _mlp(x1_s, g_mlp2, wup_b, wdn_b, g_fin2, tm=n)

    y_prompt = y_p.reshape(b, t, D_MODEL)
    y_sample = y_s.reshape(n, 1, D_MODEL)
    new_k_prompt = k_p.reshape(1, b, t, N_HEADS, HEAD_W)
    new_v_prompt = v_p.reshape(1, b, t, N_HEADS, HEAD_W)
    new_pool_prompt = u_p.reshape(b, t, D_POOL)[:, t - POOL_BUF:, :][None]
    new_k_sample = k_s.reshape(1, n, 1, N_HEADS, HEAD_W)
    new_v_sample = v_s.reshape(1, n, 1, N_HEADS, HEAD_W)
    new_pool_sample = jnp.concatenate([state_pool[layer][:, 1:, :], u_s[:, None, :]], axis=1)[None]
    return (y_prompt, y_sample, new_k_prompt, new_v_prompt, new_pool_prompt,
            new_k_sample, new_v_sample, new_pool_sample)
```

```python
import functools
import math

import jax
import jax.numpy as jnp
from jax import lax
from jax.experimental import pallas as pl
from jax.experimental.pallas import tpu as pltpu

D_MODEL = 1024
N_HEADS = 8
HEAD_DIM = 64
HEAD_W = 2 * HEAD_DIM
D_ATTN = N_HEADS * HEAD_W
POOL_WINDOWS = (2, 4, 8, 16)
D_POOL = D_MODEL // 2
POOL_GROUP = D_POOL // len(POOL_WINDOWS)
POOL_BUF = max(POOL_WINDOWS) - 1
HALO = 16
D_FF = 4 * D_MODEL
ROPE_THETA = 10000.0
RMS_EPS = 1e-6
SUBLN_EPS = 1e-5
NEG_INF = -1e30
PAGE_SIZE = 128
OFF_Q = D_POOL
OFF_K = OFF_Q + D_ATTN
OFF_V = OFF_K + D_ATTN
OFF_G = OFF_V + D_ATTN
D_IN = OFF_G + 2 * D_MODEL

LANES = 128
SUBLANES = 8
VMEM_LIMIT = 56 * 1024 * 1024

BF16 = jnp.bfloat16
F32 = jnp.float32


def _lambda_init(layer):
    return 0.8 - 0.6 * math.exp(-0.3 * layer)


def _resident(shape):
    nd = len(shape)
    return pl.BlockSpec(shape, lambda *_: (0,) * nd, pipeline_mode=pl.Buffered(1))


def _slab(block, axis, c):
    index = (lambda *_: (c, 0)) if axis == 0 else (lambda *_: (0, c))
    return pl.BlockSpec(block, index, pipeline_mode=pl.Buffered(1))


def _params(*sem):
    return pltpu.CompilerParams(dimension_semantics=sem, vmem_limit_bytes=VMEM_LIMIT)


def _prep_kernel(invf_ref, lq1_ref, lk1_ref, lq2_ref, lk2_ref, cos_ref, sin_ref, lam_ref, *, n_prompt, past_len, lam_init):
    rows = cos_ref.shape[0]
    row = lax.broadcasted_iota(jnp.int32, (rows, LANES), 0)
    lane = lax.broadcasted_iota(jnp.int32, (rows, LANES), 1)
    pos = jnp.where(row < n_prompt, row, past_len).astype(F32)
    ang = pos * invf_ref[...]
    cos_ref[...] = jnp.cos(ang)
    sin_ref[...] = jnp.where((lane % HEAD_DIM) < HEAD_DIM // 2, -jnp.sin(ang), jnp.sin(ang))
    d1 = jnp.sum(lq1_ref[...] * lk1_ref[...], axis=-1, keepdims=True)
    d2 = jnp.sum(lq2_ref[...] * lk2_ref[...], axis=-1, keepdims=True)
    lam = jnp.exp(d1) - jnp.exp(d2) + lam_init
    lam_ref[...] = jnp.broadcast_to(lam, lam_ref.shape)


def _prep(lq1, lk1, lq2, lk2, *, n_prompt, n_sample_rows, past_len, lam_init):
    half = HEAD_DIM // 2
    inv_freq = ROPE_THETA ** (-jnp.arange(half, dtype=F32) * (2.0 / HEAD_DIM))
    invf = jnp.tile(inv_freq, LANES // half)[None, :]
    rows = n_prompt + n_sample_rows
    kern = functools.partial(_prep_kernel, n_prompt=n_prompt, past_len=past_len, lam_init=lam_init)
    return pl.pallas_call(
        kern,
        out_shape=(jax.ShapeDtypeStruct((rows, LANES), F32),
                   jax.ShapeDtypeStruct((rows, LANES), F32),
                   jax.ShapeDtypeStruct((SUBLANES, LANES), F32)),
        name="prep",
    )(invf, lq1[None, :], lk1[None, :], lq2[None, :], lk2[None, :])


PROJ_CHUNK = 512
Q_SCALE = HEAD_DIM ** -0.5 * math.log2(math.e)
N_W_CHUNKS = D_IN // PROJ_CHUNK


def _rope(z, cos, sin, first_half):
    swapped = jnp.where(first_half, pltpu.roll(z, LANES - HEAD_DIM // 2, 1), pltpu.roll(z, HEAD_DIM // 2, 1))
    return z * cos + swapped * sin


POOL_ROWS = 64


def _window_means(ext_scr, pos, o_ref, tm):
    for g, w in enumerate(POOL_WINDOWS):
        sl = slice(g * POOL_GROUP, (g + 1) * POOL_GROUP)
        for r0 in range(0, tm, POOL_ROWS):
            cur = ext_scr[HALO + r0:HALO + r0 + POOL_ROWS, sl]
            acc = cur
            for k in range(1, w):
                acc = acc + ext_scr[HALO + r0 - k:HALO + r0 - k + POOL_ROWS, sl]
            count = jnp.minimum(pos[r0:r0 + POOL_ROWS] + 1, w).astype(F32)
            o_ref[r0:r0 + POOL_ROWS, sl] = (acc / count - cur).astype(o_ref.dtype)


def _proj_kernel(x_ref, g_ref, *refs, tiles_per_seq, before_main=None):
    w_refs = refs[:N_W_CHUNKS]
    cos_ref, sin_ref, u_ref, q_ref, k_ref, v_ref, kb_ref, vb_ref, gate_ref = refs[N_W_CHUNKS:N_W_CHUNKS + 9]
    rest = refs[N_W_CHUNKS + 9:]
    if tiles_per_seq is None:
        (h_scr,) = rest
    else:
        pooled_ref, h_scr, ext_scr = rest
    tm = x_ref.shape[0]
    if tiles_per_seq is not None:
        i = pl.program_id(0) % tiles_per_seq

        @pl.when(i == 0)
        def _():
            ext_scr[0:HALO, :] = jnp.zeros((HALO, D_POOL), F32)

        @pl.when(i > 0)
        def _():
            ext_scr[0:HALO, :] = ext_scr[tm:tm + HALO, :]

    if before_main is not None:
        before_main()
    x = x_ref[...]
    ms = jnp.mean(x * x, axis=-1, keepdims=True)
    h_scr[...] = (x * lax.rsqrt(ms + RMS_EPS) * g_ref[...]).astype(BF16)
    cos = cos_ref[...]
    sin = sin_ref[...]
    lane = lax.broadcasted_iota(jnp.int32, (tm, LANES), 1)
    first_half = (lane % HEAD_DIM) < HEAD_DIM // 2

    def dot(off, width):
        assert off % PROJ_CHUNK == 0 and width == PROJ_CHUNK
        return jnp.dot(h_scr[...], w_refs[off // PROJ_CHUNK][...], preferred_element_type=F32)

    u = dot(0, D_POOL)
    u_ref[...] = u
    if tiles_per_seq is not None:
        ext_scr[HALO:HALO + tm, :] = u
    for c in range(D_ATTN // PROJ_CHUNK):
        zq = dot(OFF_Q + c * PROJ_CHUNK, PROJ_CHUNK)
        zk = dot(OFF_K + c * PROJ_CHUNK, PROJ_CHUNK)
        for j in range(PROJ_CHUNK // LANES):
            sl = slice(j * LANES, (j + 1) * LANES)
            osl = slice(c * PROJ_CHUNK + j * LANES, c * PROJ_CHUNK + (j + 1) * LANES)
            q_ref[:, osl] = (_rope(zq[:, sl], cos, sin, first_half) * Q_SCALE).astype(BF16)
            kr = _rope(zk[:, sl], cos, sin, first_half)
            k_ref[:, osl] = kr
            kb_ref[:, osl] = kr.astype(BF16)
        zv = dot(OFF_V + c * PROJ_CHUNK, PROJ_CHUNK)
        csl = slice(c * PROJ_CHUNK, (c + 1) * PROJ_CHUNK)
        v_ref[:, csl] = zv
        vb_ref[:, csl] = zv.astype(BF16)
    if tiles_per_seq is not None:
        _window_means(ext_scr, i * tm + lax.broadcasted_iota(jnp.int32, (tm, POOL_GROUP), 0), pooled_ref, tm)
    for c in range(2 * D_MODEL // PROJ_CHUNK):
        zg = dot(OFF_G + c * PROJ_CHUNK, PROJ_CHUNK)
        gate_ref[:, c * PROJ_CHUNK:(c + 1) * PROJ_CHUNK] = jax.nn.sigmoid(zg)


def _proj_io(m, tm, tab_block_of, tiles_per_seq):
    row = lambda i, *_: (i, 0)
    tab = pl.BlockSpec((tm, LANES), lambda i, *_: (tab_block_of(i), 0))
    outs = [
        jax.ShapeDtypeStruct((m, D_POOL), F32),
        jax.ShapeDtypeStruct((m, D_ATTN), BF16),
        jax.ShapeDtypeStruct((m, D_ATTN), F32),
        jax.ShapeDtypeStruct((m, D_ATTN), F32),
        jax.ShapeDtypeStruct((m, D_ATTN), BF16),
        jax.ShapeDtypeStruct((m, D_ATTN), BF16),
        jax.ShapeDtypeStruct((m, 2 * D_MODEL), F32),
    ]
    scratch = [pltpu.VMEM((tm, D_MODEL), BF16)]
    if tiles_per_seq is not None:
        outs.append(jax.ShapeDtypeStruct((m, D_POOL), BF16))
        scratch.append(pltpu.VMEM((HALO + tm, D_POOL), F32))
    in_specs = ([pl.BlockSpec((tm, D_MODEL), row), _resident((1, D_MODEL))]
                + [_slab((D_MODEL, PROJ_CHUNK), 1, c) for c in range(N_W_CHUNKS)] + [tab, tab])
    out_specs = [pl.BlockSpec((tm, s.shape[1]), row) for s in outs]
    return in_specs, out_specs, outs, scratch


def _proj(x2d, g_mix, w_in_b, cos_tab, sin_tab, *, tm, tab_block_of, tiles_per_seq=None):
    m = x2d.shape[0]
    in_specs, out_specs, outs, scratch = _proj_io(m, tm, tab_block_of, tiles_per_seq)
    return pl.pallas_call(
        functools.partial(_proj_kernel, tiles_per_seq=tiles_per_seq),
        grid=(m // tm,),
        in_specs=in_specs,
        out_specs=tuple(out_specs),
        out_shape=tuple(outs),
        scratch_shapes=scratch,
        compiler_params=_params("parallel" if tiles_per_seq is None else "arbitrary"),
        name="proj",
    )(x2d, g_mix, *([w_in_b] * N_W_CHUNKS), cos_tab, sin_tab)


def _pool_sample_kernel(state_ref, u_ref, o_ref, *, past_len):
    for g, w in enumerate(POOL_WINDOWS):
        sl = slice(g * POOL_GROUP, (g + 1) * POOL_GROUP)
        cur = u_ref[:, sl]
        acc = cur
        for k in range(1, w):
            acc = acc + state_ref[POOL_BUF - k, :, sl]
        count = float(min(past_len + 1, w))
        o_ref[:, sl] = (acc / count - cur).astype(o_ref.dtype)


def _pool_sample(state_t, u_s, *, past_len):
    n = u_s.shape[0]
    return pl.pallas_call(
        functools.partial(_pool_sample_kernel, past_len=past_len),
        out_shape=jax.ShapeDtypeStruct((n, D_POOL), BF16),
        name="pool_sample",
    )(state_t, u_s)


def _subln(o, g, lam_init):
    ms = jnp.mean(o * o, axis=-1, keepdims=True)
    return o * lax.rsqrt(ms + SUBLN_EPS) * g * (1.0 - lam_init)


ONES_ROWS = 16
VT_ROWS = HEAD_W + ONES_ROWS
COL_STRIP = 256


def _attn_prompt_kernel(lam_ref, g_ref, q_ref, k_ref, v_ref, o_ref,
                        vt_scr, qt_scr, s0_scr, s1_scr, m_scr, acc_scr, *, blk, lam_init):
    nblk = k_ref.shape[0] // blk
    drow = lax.broadcasted_iota(jnp.int32, (HEAD_W, blk), 0)
    for r in range(nblk):
        rows = slice(r * blk, (r + 1) * blk)
        vt_scr[r, 0:HEAD_W, :] = v_ref[rows, :].astype(F32).T.astype(BF16)
        vt_scr[r, HEAD_W:VT_ROWS, :] = jnp.ones((ONES_ROWS, blk), BF16)
        qt = q_ref[rows, :].astype(F32).T
        qt_scr[r, 0] = jnp.where(drow < HEAD_DIM, qt, 0.0).astype(BF16)
        qt_scr[r, 1] = jnp.where(drow >= HEAD_DIM, qt, 0.0).astype(BF16)
    krow = lax.broadcasted_iota(jnp.int32, (blk, blk), 0)
    qcol = lax.broadcasted_iota(jnp.int32, (blk, blk), 1)
    causal = krow <= qcol
    lam = lam_ref[0:1, 0:1]
    gain = g_ref[...] * (1.0 - lam_init)
    s_bufs = (s0_scr, s1_scr)
    units = [(qi, j) for qi in range(nblk) for j in range(qi + 1)]

    def fill(u):
        qi, j = units[u]
        k = k_ref[j * blk:(j + 1) * blk, :]
        for c in range(2):
            s = jnp.dot(k, qt_scr[qi, c], preferred_element_type=F32)
            s_bufs[u % 2][c] = jnp.where(causal, s, NEG_INF) if j == qi else s

    def consume(u):
        qi, j = units[u]
        st = qi % 2
        vt = vt_scr[j]
        for c in range(2):
            for h in range(blk // COL_STRIP):
                cols = slice(h * COL_STRIP, (h + 1) * COL_STRIP)
                s_ref = s_bufs[u % 2].at[c, :, cols]
                m_blk = jnp.max(s_ref[...], axis=0, keepdims=True)
                if j == 0:
                    m_new = m_blk
                    p = jnp.exp2(s_ref[...] - m_new).astype(BF16)
                    acc_scr[st, c, :, cols] = jnp.dot(vt, p, preferred_element_type=F32)
                else:
                    m_prev = m_scr[st, c, :, cols]
                    m_new = jnp.maximum(m_prev, m_blk)
                    p = jnp.exp2(s_ref[...] - m_new).astype(BF16)
                    acc_scr[st, c, :, cols] = (jnp.exp2(m_prev - m_new) * acc_scr[st, c, :, cols]
                                               + jnp.dot(vt, p, preferred_element_type=F32))
                m_scr[st, c, :, cols] = m_new

    def finalize(qi):
        a1 = acc_scr[qi % 2, 0]
        a2 = acc_scr[qi % 2, 1]
        ot = (a1[0:HEAD_W] * (1.0 / a1[HEAD_W:HEAD_W + 1])
              - lam * (a2[0:HEAD_W] * (1.0 / a2[HEAD_W:HEAD_W + 1])))
        ms = jnp.mean(ot * ot, axis=0, keepdims=True)
        on = (ot * lax.rsqrt(ms + SUBLN_EPS)).T * gain
        o_ref[qi * blk:(qi + 1) * blk, :] = on.astype(o_ref.dtype)

    fill(0)
    for u, (qi, j) in enumerate(units):
        if u + 1 < len(units):
            fill(u + 1)
        consume(u)
        if j == qi:
            finalize(qi)


def _attn_prompt(lam, g_subln, q, kb, vb, *, blk, lam_init):
    b, t, _ = q.shape
    kern = functools.partial(_attn_prompt_kernel, blk=blk, lam_init=lam_init)
    seq = pl.BlockSpec((None, t, HEAD_W), lambda bi, h: (bi, 0, h))
    return pl.pallas_call(
        kern,
        grid=(b, N_HEADS),
        in_specs=[_resident((SUBLANES, LANES)), _resident((1, HEAD_W)), seq, seq, seq],
        out_specs=seq,
        out_shape=jax.ShapeDtypeStruct((b, t, D_ATTN), BF16),
        scratch_shapes=[pltpu.VMEM((t // blk, VT_ROWS, blk), BF16), pltpu.VMEM((t // blk, 2, HEAD_W, blk), BF16),
                        pltpu.VMEM((2, blk, blk), F32), pltpu.VMEM((2, blk, blk), F32),
                        pltpu.VMEM((2, 2, 1, blk), F32), pltpu.VMEM((2, 2, VT_ROWS, blk), F32)],
        compiler_params=_params("parallel", "parallel"),
        name="attn_prompt",
    )(lam, g_subln, q, kb, vb)


N_MAPROWS = 2 * N_HEADS
HALF_TOKENS = PAGE_SIZE // 2
HALF_KEYS = HALF_TOKENS * N_HEADS
QUARTER_TOKENS = PAGE_SIZE // 4
QUARTER_KEYS = QUARTER_TOKENS * N_HEADS


def _decode_weights(q_ref, kn_ref, k_refs):
    row = lax.broadcasted_iota(jnp.int32, (N_MAPROWS, HEAD_W), 0)
    lane = lax.broadcasted_iota(jnp.int32, (N_MAPROWS, HEAD_W), 1)
    q = q_ref[...].astype(F32)
    qm = jnp.where((lane >= HEAD_DIM) == (row >= N_HEADS), jnp.concatenate([q, q], axis=0), 0.0)
    zero = jnp.zeros_like(qm)
    qm2 = jnp.concatenate([jnp.concatenate([qm, zero], axis=1),
                           jnp.concatenate([zero, qm], axis=1)], axis=0).astype(BF16)

    krow = lax.broadcasted_iota(jnp.int32, (2 * N_MAPROWS, HALF_KEYS), 0)
    kcol = lax.broadcasted_iota(jnp.int32, (2 * N_MAPROWS, HALF_KEYS), 1)
    head_bits = N_HEADS - 1
    own = (kcol & head_bits) == (krow & head_bits)

    def scores(queries, keys):
        return lax.dot_general(queries, keys, (((1,), (1,)), ((), ())), preferred_element_type=F32)

    s_pages = [jnp.where(own, scores(qm2, _page_halves(kr)), NEG_INF) for kr in k_refs]
    s_new = jnp.where(lane == (row & head_bits), scores(qm.astype(BF16), _pad_new(kn_ref)), NEG_INF)

    m2 = jnp.max(s_pages[0], axis=-1, keepdims=True)
    for s in s_pages[1:]:
        m2 = jnp.maximum(m2, jnp.max(s, axis=-1, keepdims=True))
    m = jnp.maximum(jnp.maximum(m2[0:N_MAPROWS], m2[N_MAPROWS:2 * N_MAPROWS]), jnp.max(s_new, axis=-1, keepdims=True))
    mm = jnp.concatenate([m, m], axis=0)
    p_new = jnp.exp2(s_new - m)
    l = jnp.sum(p_new, axis=-1, keepdims=True)
    l2 = jnp.zeros((2 * N_MAPROWS, 1), F32)
    p_pages = []
    for s in s_pages:
        p = jnp.exp2(s - mm)
        l2 = l2 + jnp.sum(p, axis=-1, keepdims=True)
        p_pages.append(p.astype(BF16))
    l = l + l2[0:N_MAPROWS] + l2[N_MAPROWS:2 * N_MAPROWS]
    return p_pages, p_new.astype(BF16), l


def _page_halves(ref):
    a = ref[0:HALF_TOKENS].reshape(HALF_KEYS, HEAD_W)
    b = ref[HALF_TOKENS:PAGE_SIZE].reshape(HALF_KEYS, HEAD_W)
    return jnp.concatenate([a, b], axis=1).astype(BF16)


def _pad_new(ref):
    pad = jnp.zeros((LANES - N_HEADS, HEAD_W), F32)
    return jnp.concatenate([ref[...], pad], axis=0).astype(BF16)


def _page_quarters(ref):
    parts = [ref[i * QUARTER_TOKENS:(i + 1) * QUARTER_TOKENS].reshape(QUARTER_KEYS, HEAD_W) for i in range(4)]
    return jnp.concatenate(parts, axis=1).astype(BF16)


def _decode_output(weights, lam_ref, g_ref, vn_ref, v_refs, o_ref, *, lam_init):
    p_pages, p_new, l = weights
    acc = jnp.dot(p_new, _pad_new(vn_ref), preferred_element_type=F32)
    acc4 = jnp.zeros((4 * N_MAPROWS, 4 * HEAD_W), F32)
    for p, vr in zip(p_pages, v_refs):
        p4 = jnp.concatenate([p[0:N_MAPROWS, 0:QUARTER_KEYS], p[0:N_MAPROWS, QUARTER_KEYS:HALF_KEYS],
                              p[N_MAPROWS:2 * N_MAPROWS, 0:QUARTER_KEYS],
                              p[N_MAPROWS:2 * N_MAPROWS, QUARTER_KEYS:HALF_KEYS]], axis=0)
        acc4 = acc4 + jnp.dot(p4, _page_quarters(vr), preferred_element_type=F32)
    for i in range(4):
        acc = acc + acc4[i * N_MAPROWS:(i + 1) * N_MAPROWS, i * HEAD_W:(i + 1) * HEAD_W]
    a = acc / l
    o = a[0:N_HEADS] - lam_ref[0:1, 0:1] * a[N_HEADS:N_MAPROWS]
    o_ref[...] = _subln(o, g_ref[...], lam_init).astype(o_ref.dtype)


N_PROJ_IN = 4 + N_W_CHUNKS
N_PROJ_OUT = 8
N_DEC_IN = 5


def _proj_decode_kernel(pt_ref, *refs, tiles_per_seq, n_pages, lam_init):
    del pt_ref
    proj_in = refs[:N_PROJ_IN]
    lam_ref, gs_ref, q_ref, kn_ref, vn_ref = refs[N_PROJ_IN:N_PROJ_IN + N_DEC_IN]
    base = N_PROJ_IN + N_DEC_IN
    pages = refs[base:base + 2 * n_pages]
    proj_out = refs[base + 2 * n_pages:base + 2 * n_pages + N_PROJ_OUT]
    on_ref, h_scr, ext_scr = refs[base + 2 * n_pages + N_PROJ_OUT:]
    weights = []
    _proj_kernel(*proj_in, *proj_out, h_scr, ext_scr, tiles_per_seq=tiles_per_seq,
                 before_main=lambda: weights.append(_decode_weights(q_ref, kn_ref, pages[:n_pages])))
    _decode_output(weights[0], lam_ref, gs_ref, vn_ref, pages[n_pages:], on_ref, lam_init=lam_init)


def _proj_decode(x2d, g_mix, w_in_b, cos_tab, sin_tab, page_table, lam, g_subln, q3, kn3, vn3, cache_k4, cache_v4,
                 *, tiles_per_seq, lam_init):
    m = x2d.shape[0]
    n, n_pages = page_table.shape
    tm = m // n
    assert tm * n == m and tm % HALO == 0
    in_specs, out_specs, outs, scratch = _proj_io(m, tm, lambda i: i % tiles_per_seq, tiles_per_seq)
    seq = pl.BlockSpec((None, N_HEADS, HEAD_W), lambda i, pt: (i, 0, 0))

    def page_spec(j):
        return pl.BlockSpec((None, PAGE_SIZE, N_HEADS, HEAD_W), lambda i, pt: (pt[i * n_pages + j], 0, 0, 0))

    grid_spec = pltpu.PrefetchScalarGridSpec(
        num_scalar_prefetch=1,
        grid=(n,),
        in_specs=in_specs + [_resident((SUBLANES, LANES)), _resident((1, HEAD_W)), seq, seq, seq]
                 + [page_spec(j) for j in range(n_pages)] * 2,
        out_specs=tuple(out_specs) + (seq,),
        scratch_shapes=scratch,
    )
    kern = functools.partial(_proj_decode_kernel, tiles_per_seq=tiles_per_seq, n_pages=n_pages, lam_init=lam_init)
    return pl.pallas_call(
        kern,
        grid_spec=grid_spec,
        out_shape=tuple(outs) + (jax.ShapeDtypeStruct((n, N_HEADS, HEAD_W), F32),),
        compiler_params=_params("arbitrary"),
        name="proj_decode",
    )(page_table.reshape(-1), x2d, g_mix, *([w_in_b] * N_W_CHUNKS), cos_tab, sin_tab, lam, g_subln, q3, kn3, vn3,
      *([cache_k4] * n_pages), *([cache_v4] * n_pages))


def _mix_kernel(x_ref, pooled_ref, on_ref, gate_ref, wg_ref, scale_ref, wpp_ref, wap_ref, wo_ref, o_ref):
    parts = []
    for g in range(len(POOL_WINDOWS)):
        sl = slice(g * POOL_GROUP, (g + 1) * POOL_GROUP)
        parts.append(jnp.dot(pooled_ref[:, sl], wg_ref[g], preferred_element_type=F32))
    pool_out = jnp.concatenate(parts, axis=1) * scale_ref[...]
    pool_proj = jnp.dot(pool_out.astype(BF16), wpp_ref[...], preferred_element_type=F32)
    attn_proj = jnp.dot(on_ref[...].astype(BF16), wap_ref[...], preferred_element_type=F32)
    merged = gate_ref[:, 0:D_MODEL] * pool_proj + gate_ref[:, D_MODEL:2 * D_MODEL] * attn_proj
    o_ref[...] = x_ref[...] + jnp.dot(merged.astype(BF16), wo_ref[...], preferred_element_type=F32)


def _mix(x2d, pooled, on, gates, wg_b, pool_scale, wpp_b, wap_b, wo_b, *, tm):
    m = x2d.shape[0]
    row = lambda i: (i, 0)
    return pl.pallas_call(
        _mix_kernel,
        grid=(m // tm,),
        in_specs=[pl.BlockSpec((tm, D_MODEL), row), pl.BlockSpec((tm, D_POOL), row),
                  pl.BlockSpec((tm, D_ATTN), row), pl.BlockSpec((tm, 2 * D_MODEL), row),
                  _resident(wg_b.shape), _resident((1, D_POOL)), _resident(wpp_b.shape),
                  _resident(wap_b.shape), _resident(wo_b.shape)],
        out_specs=pl.BlockSpec((tm, D_MODEL), row),
        out_shape=jax.ShapeDtypeStruct((m, D_MODEL), F32),
        compiler_params=_params("parallel"),
        name="mix",
    )(x2d, pooled, on, gates, wg_b, pool_scale, wpp_b, wap_b, wo_b)


MLP_CHUNK = 1024


def _rms(x, g):
    ms = jnp.mean(x * x, axis=-1, keepdims=True)
    return x * lax.rsqrt(ms + RMS_EPS) * g


N_FF_CHUNKS = D_FF // MLP_CHUNK


def _mlp_kernel(x_ref, g_ref, *refs):
    wup_refs = refs[:N_FF_CHUNKS]
    wdn_refs = refs[N_FF_CHUNKS:2 * N_FF_CHUNKS]
    gf_ref, o_ref, h_scr = refs[2 * N_FF_CHUNKS:]
    x = x_ref[...]
    h_scr[...] = _rms(x, g_ref[...]).astype(BF16)
    y = x
    for c in range(N_FF_CHUNKS):
        a = jnp.maximum(jnp.dot(h_scr[...], wup_refs[c][...], preferred_element_type=F32), 0.0)
        y = y + jnp.dot((a * a).astype(BF16), wdn_refs[c][...], preferred_element_type=F32)
    o_ref[...] = _rms(y, gf_ref[...])


def _mlp_weight_specs():
    return ([_slab((D_MODEL, MLP_CHUNK), 1, c) for c in range(N_FF_CHUNKS)]
            + [_slab((MLP_CHUNK, D_MODEL), 0, c) for c in range(N_FF_CHUNKS)])


def _mlp(x2d, g_mlp, wup_b, wdn_b, g_final, *, tm):
    m = x2d.shape[0]
    row = lambda i: (i, 0)
    return pl.pallas_call(
        _mlp_kernel,
        grid=(m // tm,),
        in_specs=[pl.BlockSpec((tm, D_MODEL), row), _resident((1, D_MODEL))] + _mlp_weight_specs()
                 + [_resident((1, D_MODEL))],
        out_specs=pl.BlockSpec((tm, D_MODEL), row),
        out_shape=jax.ShapeDtypeStruct((m, D_MODEL), F32),
        scratch_shapes=[pltpu.VMEM((tm, D_MODEL), BF16)],
        compiler_params=_params("parallel"),
        name="mlp",
    )(x2d, g_mlp, *([wup_b] * N_FF_CHUNKS), *([wdn_b] * N_FF_CHUNKS), g_final)


PROMPT_TM = 512
ATTN_BLK = 512


def kernel(x_prompt, x_sample, state_pool, cache_k, cache_v, page_table, g_mix, w_in, w_pool_grp, pool_scale,
           w_pool_proj, lambda_q1, lambda_k1, lambda_q2, lambda_k2, g_subln, w_attn_proj, w_o, g_mlp, w_up,
           w_down, g_final):
    depth = w_in.shape[0]
    assert depth == 1, "single-layer trunk"
    layer = 0
    lam_init = _lambda_init(layer)
    b, t, _ = x_prompt.shape
    n, t_s, _ = x_sample.shape
    assert t_s == 1 and t % PROMPT_TM == 0 and t % ATTN_BLK == 0
    past_len = page_table.shape[1] * PAGE_SIZE

    w_in_b = w_in[layer].astype(BF16)
    wg_b = w_pool_grp[layer].astype(BF16)
    wpp_b = w_pool_proj[layer].astype(BF16)
    wap_b = w_attn_proj[layer].astype(BF16)
    wo_b = w_o[layer].astype(BF16)
    wup_b = w_up[layer].astype(BF16)
    wdn_b = w_down[layer].astype(BF16)
    g_mix2 = g_mix[layer][None, :]
    g_mlp2 = g_mlp[layer][None, :]
    g_fin2 = g_final[None, :]
    g_sub2 = g_subln[layer][None, :]
    scale2 = pool_scale[layer][None, :]

    cos_tab, sin_tab, lam = _prep(lambda_q1[layer], lambda_k1[layer], lambda_q2[layer], lambda_k2[layer],
                                  n_prompt=t, n_sample_rows=n, past_len=past_len, lam_init=lam_init)

    xs2 = x_sample.reshape(n, D_MODEL)
    u_s, q_s, k_s, v_s, _, _, gates_s = _proj(
        xs2, g_mix2, w_in_b, cos_tab, sin_tab, tm=n, tab_block_of=lambda i: t // n)
    xp2 = x_prompt.reshape(b * t, D_MODEL)
    heads = lambda a: a.reshape(n, N_HEADS, HEAD_W)
    rows_per_step = b * t // n
    assert rows_per_step * n == b * t and t % rows_per_step == 0
    u_p, q_p, k_p, v_p, kb_p, vb_p, gates_p, pooled_p, on_s = _proj_decode(
        xp2, g_mix2, w_in_b, cos_tab, sin_tab, page_table, lam, g_sub2, heads(q_s), heads(k_s), heads(v_s),
        cache_k[layer], cache_v[layer], tiles_per_seq=t // rows_per_step, lam_init=lam_init)

    on_p = _attn_prompt(lam, g_sub2, q_p.reshape(b, t, D_ATTN), kb_p.reshape(b, t, D_ATTN),
                        vb_p.reshape(b, t, D_ATTN), blk=ATTN_BLK, lam_init=lam_init)
    x1_p = _mix(xp2, pooled_p, on_p.reshape(b * t, D_ATTN), gates_p, wg_b, scale2, wpp_b, wap_b, wo_b,
                tm=PROMPT_TM)
    y_p = _mlp(x1_p, g_mlp2, wup_b, wdn_b, g_fin2, tm=PROMPT_TM)

    state_t = jnp.swapaxes(state_pool[layer], 0, 1)
    pooled_s = _pool_sample(state_t, u_s, past_len=past_len)
    x1_s = _mix(xs2, pooled_s, on_s.reshape(n, D_ATTN), gates_s, wg_b, scale2, wpp_b, wap_b, wo_b, tm=n)
    y_s = _mlp(x1_s, g_mlp2, wup_b, wdn_b, g_fin2, tm=n)

    y_prompt = y_p.reshape(b, t, D_MODEL)
    y_sample = y_s.reshape(n, 1, D_MODEL)
    new_k_prompt = k_p.reshape(1, b, t, N_HEADS, HEAD_W)
    new_v_prompt = v_p.reshape(1, b, t, N_HEADS, HEAD_W)
    new_pool_prompt = u_p.reshape(b, t, D_POOL)[:, t - POOL_BUF:, :][None]
    new_k_sample = k_s.reshape(1, n, 1, N_HEADS, HEAD_W)
    new_v_sample = v_s.reshape(1, n, 1, N_HEADS, HEAD_W)
    new_pool_sample = jnp.concatenate([state_pool[layer][:, 1:, :], u_s[:, None, :]], axis=1)[None]
    return (y_prompt, y_sample, new_k_prompt, new_v_prompt, new_pool_prompt,
            new_k_sample, new_v_sample, new_pool_sample)
```

```python
import functools
import math

import jax
import jax.numpy as jnp
from jax import lax
from jax.experimental import pallas as pl
from jax.experimental.pallas import tpu as pltpu

D_MODEL = 1024
N_HEADS = 8
HEAD_DIM = 64
HEAD_W = 2 * HEAD_DIM
D_ATTN = N_HEADS * HEAD_W
POOL_WINDOWS = (2, 4, 8, 16)
D_POOL = D_MODEL // 2
POOL_GROUP = D_POOL // len(POOL_WINDOWS)
POOL_BUF = max(POOL_WINDOWS) - 1
HALO = 16
D_FF = 4 * D_MODEL
ROPE_THETA = 10000.0
RMS_EPS = 1e-6
SUBLN_EPS = 1e-5
NEG_INF = -1e30
PAGE_SIZE = 128
OFF_Q = D_POOL
OFF_K = OFF_Q + D_ATTN
OFF_V = OFF_K + D_ATTN
OFF_G = OFF_V + D_ATTN
D_IN = OFF_G + 2 * D_MODEL

LANES = 128
SUBLANES = 8
VMEM_LIMIT = 56 * 1024 * 1024

BF16 = jnp.bfloat16
F32 = jnp.float32


def _lambda_init(layer):
    return 0.8 - 0.6 * math.exp(-0.3 * layer)


def _resident(shape):
    nd = len(shape)
    return pl.BlockSpec(shape, lambda *_: (0,) * nd, pipeline_mode=pl.Buffered(1))


def _params(*sem):
    return pltpu.CompilerParams(dimension_semantics=sem, vmem_limit_bytes=VMEM_LIMIT)


def _prep_kernel(invf_ref, lq1_ref, lk1_ref, lq2_ref, lk2_ref, cos_ref, sin_ref, lam_ref, *, n_prompt, past_len, lam_init):
    rows = cos_ref.shape[0]
    row = lax.broadcasted_iota(jnp.int32, (rows, LANES), 0)
    lane = lax.broadcasted_iota(jnp.int32, (rows, LANES), 1)
    pos = jnp.where(row < n_prompt, row, past_len).astype(F32)
    ang = pos * invf_ref[...]
    cos_ref[...] = jnp.cos(ang)
    sin_ref[...] = jnp.where((lane % HEAD_DIM) < HEAD_DIM // 2, -jnp.sin(ang), jnp.sin(ang))
    d1 = jnp.sum(lq1_ref[...] * lk1_ref[...], axis=-1, keepdims=True)
    d2 = jnp.sum(lq2_ref[...] * lk2_ref[...], axis=-1, keepdims=True)
    lam = jnp.exp(d1) - jnp.exp(d2) + lam_init
    lam_ref[...] = jnp.broadcast_to(lam, lam_ref.shape)


def _prep(lq1, lk1, lq2, lk2, *, n_prompt, n_sample_rows, past_len, lam_init):
    half = HEAD_DIM // 2
    inv_freq = ROPE_THETA ** (-jnp.arange(half, dtype=F32) * (2.0 / HEAD_DIM))
    invf = jnp.tile(inv_freq, LANES // half)[None, :]
    rows = n_prompt + n_sample_rows
    kern = functools.partial(_prep_kernel, n_prompt=n_prompt, past_len=past_len, lam_init=lam_init)
    return pl.pallas_call(
        kern,
        out_shape=(jax.ShapeDtypeStruct((rows, LANES), F32),
                   jax.ShapeDtypeStruct((rows, LANES), F32),
                   jax.ShapeDtypeStruct((SUBLANES, LANES), F32)),
        name="prep",
    )(invf, lq1[None, :], lk1[None, :], lq2[None, :], lk2[None, :])


PROJ_CHUNK = 512
Q_SCALE = HEAD_DIM ** -0.5 * math.log2(math.e)


def _rope(z, cos, sin, first_half):
    swapped = jnp.where(first_half, pltpu.roll(z, LANES - HEAD_DIM // 2, 1), pltpu.roll(z, HEAD_DIM // 2, 1))
    return z * cos + swapped * sin


POOL_ROWS = 64


def _window_means(ext_scr, pos, o_ref, tm):
    for g, w in enumerate(POOL_WINDOWS):
        sl = slice(g * POOL_GROUP, (g + 1) * POOL_GROUP)
        for r0 in range(0, tm, POOL_ROWS):
            cur = ext_scr[HALO + r0:HALO + r0 + POOL_ROWS, sl]
            acc = cur
            for k in range(1, w):
                acc = acc + ext_scr[HALO + r0 - k:HALO + r0 - k + POOL_ROWS, sl]
            count = jnp.minimum(pos[r0:r0 + POOL_ROWS] + 1, w).astype(F32)
            o_ref[r0:r0 + POOL_ROWS, sl] = (acc / count - cur).astype(o_ref.dtype)


def _proj_kernel(x_ref, g_ref, w_ref, cos_ref, sin_ref,
                 u_ref, q_ref, k_ref, v_ref, kb_ref, vb_ref, gate_ref, *rest, tiles_per_seq):
    if tiles_per_seq is None:
        (h_scr,) = rest
    else:
        pooled_ref, h_scr, ext_scr = rest
    tm = x_ref.shape[0]
    if tiles_per_seq is not None:
        i = pl.program_id(0) % tiles_per_seq

        @pl.when(i == 0)
        def _():
            ext_scr[0:HALO, :] = jnp.zeros((HALO, D_POOL), F32)

        @pl.when(i > 0)
        def _():
            ext_scr[0:HALO, :] = ext_scr[tm:tm + HALO, :]

    x = x_ref[...]
    ms = jnp.mean(x * x, axis=-1, keepdims=True)
    h_scr[...] = (x * lax.rsqrt(ms + RMS_EPS) * g_ref[...]).astype(BF16)
    cos = cos_ref[...]
    sin = sin_ref[...]
    lane = lax.broadcasted_iota(jnp.int32, (tm, LANES), 1)
    first_half = (lane % HEAD_DIM) < HEAD_DIM // 2

    def dot(off, width):
        return jnp.dot(h_scr[...], w_ref[:, off:off + width], preferred_element_type=F32)

    u = dot(0, D_POOL)
    u_ref[...] = u
    if tiles_per_seq is not None:
        ext_scr[HALO:HALO + tm, :] = u
    for c in range(D_ATTN // PROJ_CHUNK):
        zq = dot(OFF_Q + c * PROJ_CHUNK, PROJ_CHUNK)
        zk = dot(OFF_K + c * PROJ_CHUNK, PROJ_CHUNK)
        for j in range(PROJ_CHUNK // LANES):
            sl = slice(j * LANES, (j + 1) * LANES)
            osl = slice(c * PROJ_CHUNK + j * LANES, c * PROJ_CHUNK + (j + 1) * LANES)
            q_ref[:, osl] = (_rope(zq[:, sl], cos, sin, first_half) * Q_SCALE).astype(BF16)
            kr = _rope(zk[:, sl], cos, sin, first_half)
            k_ref[:, osl] = kr
            kb_ref[:, osl] = kr.astype(BF16)
        zv = dot(OFF_V + c * PROJ_CHUNK, PROJ_CHUNK)
        csl = slice(c * PROJ_CHUNK, (c + 1) * PROJ_CHUNK)
        v_ref[:, csl] = zv
        vb_ref[:, csl] = zv.astype(BF16)
    if tiles_per_seq is not None:
        _window_means(ext_scr, i * tm + lax.broadcasted_iota(jnp.int32, (tm, POOL_GROUP), 0), pooled_ref, tm)
    for c in range(2 * D_MODEL // PROJ_CHUNK):
        zg = dot(OFF_G + c * PROJ_CHUNK, PROJ_CHUNK)
        gate_ref[:, c * PROJ_CHUNK:(c + 1) * PROJ_CHUNK] = jax.nn.sigmoid(zg)


def _proj(x2d, g_mix, w_in_b, cos_tab, sin_tab, *, tm, tab_block_of, tiles_per_seq=None):
    m = x2d.shape[0]
    row = lambda i: (i, 0)
    tab = pl.BlockSpec((tm, LANES), lambda i: (tab_block_of(i), 0))
    outs = [
        jax.ShapeDtypeStruct((m, D_POOL), F32),
        jax.ShapeDtypeStruct((m, D_ATTN), BF16),
        jax.ShapeDtypeStruct((m, D_ATTN), F32),
        jax.ShapeDtypeStruct((m, D_ATTN), F32),
        jax.ShapeDtypeStruct((m, D_ATTN), BF16),
        jax.ShapeDtypeStruct((m, D_ATTN), BF16),
        jax.ShapeDtypeStruct((m, 2 * D_MODEL), F32),
    ]
    scratch = [pltpu.VMEM((tm, D_MODEL), BF16)]
    if tiles_per_seq is not None:
        outs.append(jax.ShapeDtypeStruct((m, D_POOL), BF16))
        scratch.append(pltpu.VMEM((HALO + tm, D_POOL), F32))
    return pl.pallas_call(
        functools.partial(_proj_kernel, tiles_per_seq=tiles_per_seq),
        grid=(m // tm,),
        in_specs=[pl.BlockSpec((tm, D_MODEL), row), _resident((1, D_MODEL)), _resident((D_MODEL, D_IN)), tab, tab],
        out_specs=tuple(pl.BlockSpec((tm, s.shape[1]), row) for s in outs),
        out_shape=tuple(outs),
        scratch_shapes=scratch,
        compiler_params=_params("parallel" if tiles_per_seq is None else "arbitrary"),
        name="proj",
    )(x2d, g_mix, w_in_b, cos_tab, sin_tab)


def _pool_sample_kernel(state_ref, u_ref, o_ref, *, past_len):
    for g, w in enumerate(POOL_WINDOWS):
        sl = slice(g * POOL_GROUP, (g + 1) * POOL_GROUP)
        cur = u_ref[:, sl]
        acc = cur
        for k in range(1, w):
            acc = acc + state_ref[POOL_BUF - k, :, sl]
        count = float(min(past_len + 1, w))
        o_ref[:, sl] = (acc / count - cur).astype(o_ref.dtype)


def _pool_sample(state_t, u_s, *, past_len):
    n = u_s.shape[0]
    return pl.pallas_call(
        functools.partial(_pool_sample_kernel, past_len=past_len),
        out_shape=jax.ShapeDtypeStruct((n, D_POOL), BF16),
        name="pool_sample",
    )(state_t, u_s)


def _subln(o, g, lam_init):
    ms = jnp.mean(o * o, axis=-1, keepdims=True)
    return o * lax.rsqrt(ms + SUBLN_EPS) * g * (1.0 - lam_init)


ONES_ROWS = 16
VT_ROWS = HEAD_W + ONES_ROWS
COL_STRIP = 256


def _attn_prompt_kernel(lam_ref, g_ref, q_ref, k_ref, v_ref, o_ref,
                        vt_scr, qt_scr, s0_scr, s1_scr, m_scr, acc_scr, *, blk, lam_init):
    nblk = k_ref.shape[0] // blk
    drow = lax.broadcasted_iota(jnp.int32, (HEAD_W, blk), 0)
    for r in range(nblk):
        rows = slice(r * blk, (r + 1) * blk)
        vt_scr[r, 0:HEAD_W, :] = v_ref[rows, :].astype(F32).T.astype(BF16)
        vt_scr[r, HEAD_W:VT_ROWS, :] = jnp.ones((ONES_ROWS, blk), BF16)
        qt = q_ref[rows, :].astype(F32).T
        qt_scr[r, 0] = jnp.where(drow < HEAD_DIM, qt, 0.0).astype(BF16)
        qt_scr[r, 1] = jnp.where(drow >= HEAD_DIM, qt, 0.0).astype(BF16)
    lam = lam_ref[0:1, 0:1]
    gain = g_ref[...] * (1.0 - lam_init)
    s_bufs = (s0_scr, s1_scr)
    units = [(qi, j) for qi in range(nblk) for j in range(qi + 1)]
    strips = [slice(h * COL_STRIP, (h + 1) * COL_STRIP) for h in range(blk // COL_STRIP)]

    def n_keys(qi, j, h):
        return (h + 1) * COL_STRIP if j == qi else blk

    def fill(u):
        qi, j = units[u]
        for c in range(2):
            if j != qi:
                s_bufs[u % 2][c] = jnp.dot(k_ref[j * blk:(j + 1) * blk, :], qt_scr[qi, c],
                                           preferred_element_type=F32)
                continue
            for h, cols in enumerate(strips):
                nk = n_keys(qi, j, h)
                s = jnp.dot(k_ref[j * blk:j * blk + nk, :], qt_scr[qi, c, :, cols], preferred_element_type=F32)
                krow = lax.broadcasted_iota(jnp.int32, (nk, COL_STRIP), 0)
                qcol = lax.broadcasted_iota(jnp.int32, (nk, COL_STRIP), 1) + h * COL_STRIP
                s_bufs[u % 2][c, 0:nk, cols] = jnp.where(krow <= qcol, s, NEG_INF)

    def consume(u):
        qi, j = units[u]
        st = qi % 2
        for c in range(2):
            for h, cols in enumerate(strips):
                nk = n_keys(qi, j, h)
                vt = vt_scr[j, :, 0:nk]
                s_ref = s_bufs[u % 2].at[c, 0:nk, cols]
                m_blk = jnp.max(s_ref[...], axis=0, keepdims=True)
                if j == 0:
                    m_new = m_blk
                    p = jnp.exp2(s_ref[...] - m_new).astype(BF16)
                    acc_scr[st, c, :, cols] = jnp.dot(vt, p, preferred_element_type=F32)
                else:
                    m_prev = m_scr[st, c, :, cols]
                    m_new = jnp.maximum(m_prev, m_blk)
                    p = jnp.exp2(s_ref[...] - m_new).astype(BF16)
                    acc_scr[st, c, :, cols] = (jnp.exp2(m_prev - m_new) * acc_scr[st, c, :, cols]
                                               + jnp.dot(vt, p, preferred_element_type=F32))
                m_scr[st, c, :, cols] = m_new

    def finalize(qi):
        a1 = acc_scr[qi % 2, 0]
        a2 = acc_scr[qi % 2, 1]
        ot = (a1[0:HEAD_W] * (1.0 / a1[HEAD_W:HEAD_W + 1])
              - lam * (a2[0:HEAD_W] * (1.0 / a2[HEAD_W:HEAD_W + 1])))
        ms = jnp.mean(ot * ot, axis=0, keepdims=True)
        on = (ot * lax.rsqrt(ms + SUBLN_EPS)).T * gain
        o_ref[qi * blk:(qi + 1) * blk, :] = on.astype(o_ref.dtype)

    fill(0)
    for u, (qi, j) in enumerate(units):
        if u + 1 < len(units):
            fill(u + 1)
        consume(u)
        if j == qi:
            finalize(qi)


def _attn_prompt(lam, g_subln, q, kb, vb, *, blk, lam_init):
    b, t, _ = q.shape
    kern = functools.partial(_attn_prompt_kernel, blk=blk, lam_init=lam_init)
    seq = pl.BlockSpec((None, t, HEAD_W), lambda bi, h: (bi, 0, h))
    return pl.pallas_call(
        kern,
        grid=(b, N_HEADS),
        in_specs=[_resident((SUBLANES, LANES)), _resident((1, HEAD_W)), seq, seq, seq],
        out_specs=seq,
        out_shape=jax.ShapeDtypeStruct((b, t, D_ATTN), BF16),
        scratch_shapes=[pltpu.VMEM((t // blk, VT_ROWS, blk), BF16), pltpu.VMEM((t // blk, 2, HEAD_W, blk), BF16),
                        pltpu.VMEM((2, blk, blk), F32), pltpu.VMEM((2, blk, blk), F32),
                        pltpu.VMEM((2, 2, 1, blk), F32), pltpu.VMEM((2, 2, VT_ROWS, blk), F32)],
        compiler_params=_params("parallel", "parallel"),
        name="attn_prompt",
    )(lam, g_subln, q, kb, vb)


N_MAPROWS = 2 * N_HEADS
HALF_TOKENS = PAGE_SIZE // 2
HALF_KEYS = HALF_TOKENS * N_HEADS
QUARTER_TOKENS = PAGE_SIZE // 4
QUARTER_KEYS = QUARTER_TOKENS * N_HEADS


def _decode_weights(q_ref, kn_ref, k_refs):
    row = lax.broadcasted_iota(jnp.int32, (N_MAPROWS, HEAD_W), 0)
    lane = lax.broadcasted_iota(jnp.int32, (N_MAPROWS, HEAD_W), 1)
    q = q_ref[...].astype(F32)
    qm = jnp.where((lane >= HEAD_DIM) == (row >= N_HEADS), jnp.concatenate([q, q], axis=0), 0.0)
    zero = jnp.zeros_like(qm)
    qm2 = jnp.concatenate([jnp.concatenate([qm, zero], axis=1),
                           jnp.concatenate([zero, qm], axis=1)], axis=0).astype(BF16)

    krow = lax.broadcasted_iota(jnp.int32, (2 * N_MAPROWS, HALF_KEYS), 0)
    kcol = lax.broadcasted_iota(jnp.int32, (2 * N_MAPROWS, HALF_KEYS), 1)
    head_bits = N_HEADS - 1
    own = (kcol & head_bits) == (krow & head_bits)

    def scores(queries, keys):
        return lax.dot_general(queries, keys, (((1,), (1,)), ((), ())), preferred_element_type=F32)

    s_pages = [jnp.where(own, scores(qm2, _page_halves(kr)), NEG_INF) for kr in k_refs]
    s_new = jnp.where(lane == (row & head_bits), scores(qm.astype(BF16), _pad_new(kn_ref)), NEG_INF)

    m2 = jnp.max(s_pages[0], axis=-1, keepdims=True)
    for s in s_pages[1:]:
        m2 = jnp.maximum(m2, jnp.max(s, axis=-1, keepdims=True))
    m = jnp.maximum(jnp.maximum(m2[0:N_MAPROWS], m2[N_MAPROWS:2 * N_MAPROWS]), jnp.max(s_new, axis=-1, keepdims=True))
    mm = jnp.concatenate([m, m], axis=0)
    p_new = jnp.exp2(s_new - m)
    l = jnp.sum(p_new, axis=-1, keepdims=True)
    l2 = jnp.zeros((2 * N_MAPROWS, 1), F32)
    p_pages = []
    for s in s_pages:
        p = jnp.exp2(s - mm)
        l2 = l2 + jnp.sum(p, axis=-1, keepdims=True)
        p_pages.append(p.astype(BF16))
    l = l + l2[0:N_MAPROWS] + l2[N_MAPROWS:2 * N_MAPROWS]
    return p_pages, p_new.astype(BF16), l


def _page_halves(ref):
    a = ref[0:HALF_TOKENS].reshape(HALF_KEYS, HEAD_W)
    b = ref[HALF_TOKENS:PAGE_SIZE].reshape(HALF_KEYS, HEAD_W)
    return jnp.concatenate([a, b], axis=1).astype(BF16)


def _pad_new(ref):
    pad = jnp.zeros((LANES - N_HEADS, HEAD_W), F32)
    return jnp.concatenate([ref[...], pad], axis=0).astype(BF16)


def _page_quarters(ref):
    parts = [ref[i * QUARTER_TOKENS:(i + 1) * QUARTER_TOKENS].reshape(QUARTER_KEYS, HEAD_W) for i in range(4)]
    return jnp.concatenate(parts, axis=1).astype(BF16)


def _decode_output(weights, lam_ref, g_ref, vn_ref, v_refs, o_ref, *, lam_init):
    p_pages, p_new, l = weights
    acc = jnp.dot(p_new, _pad_new(vn_ref), preferred_element_type=F32)
    acc4 = jnp.zeros((4 * N_MAPROWS, 4 * HEAD_W), F32)
    for p, vr in zip(p_pages, v_refs):
        p4 = jnp.concatenate([p[0:N_MAPROWS, 0:QUARTER_KEYS], p[0:N_MAPROWS, QUARTER_KEYS:HALF_KEYS],
                              p[N_MAPROWS:2 * N_MAPROWS, 0:QUARTER_KEYS],
                              p[N_MAPROWS:2 * N_MAPROWS, QUARTER_KEYS:HALF_KEYS]], axis=0)
        acc4 = acc4 + jnp.dot(p4, _page_quarters(vr), preferred_element_type=F32)
    for i in range(4):
        acc = acc + acc4[i * N_MAPROWS:(i + 1) * N_MAPROWS, i * HEAD_W:(i + 1) * HEAD_W]
    a = acc / l
    o = a[0:N_HEADS] - lam_ref[0:1, 0:1] * a[N_HEADS:N_MAPROWS]
    o_ref[...] = _subln(o, g_ref[...], lam_init).astype(o_ref.dtype)


def _mix_kernel(x_ref, pooled_ref, on_ref, gate_ref, wg_ref, scale_ref, wpp_ref, wap_ref, wo_ref, o_ref):
    parts = []
    for g in range(len(POOL_WINDOWS)):
        sl = slice(g * POOL_GROUP, (g + 1) * POOL_GROUP)
        parts.append(jnp.dot(pooled_ref[:, sl], wg_ref[g], preferred_element_type=F32))
    pool_out = jnp.concatenate(parts, axis=1) * scale_ref[...]
    pool_proj = jnp.dot(pool_out.astype(BF16), wpp_ref[...], preferred_element_type=F32)
    attn_proj = jnp.dot(on_ref[...].astype(BF16), wap_ref[...], preferred_element_type=F32)
    merged = gate_ref[:, 0:D_MODEL] * pool_proj + gate_ref[:, D_MODEL:2 * D_MODEL] * attn_proj
    o_ref[...] = x_ref[...] + jnp.dot(merged.astype(BF16), wo_ref[...], preferred_element_type=F32)


def _mix(x2d, pooled, on, gates, wg_b, pool_scale, wpp_b, wap_b, wo_b, *, tm):
    m = x2d.shape[0]
    row = lambda i: (i, 0)
    return pl.pallas_call(
        _mix_kernel,
        grid=(m // tm,),
        in_specs=[pl.BlockSpec((tm, D_MODEL), row), pl.BlockSpec((tm, D_POOL), row),
                  pl.BlockSpec((tm, D_ATTN), row), pl.BlockSpec((tm, 2 * D_MODEL), row),
                  _resident(wg_b.shape), _resident((1, D_POOL)), _resident(wpp_b.shape),
                  _resident(wap_b.shape), _resident(wo_b.shape)],
        out_specs=pl.BlockSpec((tm, D_MODEL), row),
        out_shape=jax.ShapeDtypeStruct((m, D_MODEL), F32),
        compiler_params=_params("parallel"),
        name="mix",
    )(x2d, pooled, on, gates, wg_b, pool_scale, wpp_b, wap_b, wo_b)


MLP_CHUNK = 1024


def _rms(x, g):
    ms = jnp.mean(x * x, axis=-1, keepdims=True)
    return x * lax.rsqrt(ms + RMS_EPS) * g


def _mlp_kernel(x_ref, g_ref, wup_ref, wdn_ref, gf_ref, o_ref, h_scr):
    x = x_ref[...]
    h_scr[...] = _rms(x, g_ref[...]).astype(BF16)
    y = x
    for c in range(D_FF // MLP_CHUNK):
        sl = slice(c * MLP_CHUNK, (c + 1) * MLP_CHUNK)
        a = jnp.maximum(jnp.dot(h_scr[...], wup_ref[:, sl], preferred_element_type=F32), 0.0)
        y = y + jnp.dot((a * a).astype(BF16), wdn_ref[sl, :], preferred_element_type=F32)
    o_ref[...] = _rms(y, gf_ref[...])


def _mlp(x2d, g_mlp, wup_b, wdn_b, g_final, *, tm):
    m = x2d.shape[0]
    row = lambda i: (i, 0)
    return pl.pallas_call(
        _mlp_kernel,
        grid=(m // tm,),
        in_specs=[pl.BlockSpec((tm, D_MODEL), row), _resident((1, D_MODEL)), _resident(wup_b.shape),
                  _resident(wdn_b.shape), _resident((1, D_MODEL))],
        out_specs=pl.BlockSpec((tm, D_MODEL), row),
        out_shape=jax.ShapeDtypeStruct((m, D_MODEL), F32),
        scratch_shapes=[pltpu.VMEM((tm, D_MODEL), BF16)],
        compiler_params=_params("parallel"),
        name="mlp",
    )(x2d, g_mlp, wup_b, wdn_b, g_final)


N_MLP_IN = 5
N_DEC_IN = 5


def _mlp_decode_kernel(pt_ref, *refs, n_pages, lam_init):
    del pt_ref
    mlp_in = refs[:N_MLP_IN]
    lam_ref, gs_ref, q_ref, kn_ref, vn_ref = refs[N_MLP_IN:N_MLP_IN + N_DEC_IN]
    pages = refs[N_MLP_IN + N_DEC_IN:N_MLP_IN + N_DEC_IN + 2 * n_pages]
    y_ref, on_ref, h_scr = refs[N_MLP_IN + N_DEC_IN + 2 * n_pages:]
    weights = _decode_weights(q_ref, kn_ref, pages[:n_pages])
    _mlp_kernel(*mlp_in, y_ref, h_scr)
    _decode_output(weights, lam_ref, gs_ref, vn_ref, pages[n_pages:], on_ref, lam_init=lam_init)


def _mlp_decode(x2d, g_mlp, wup_b, wdn_b, g_final, page_table, lam, g_subln, q3, kn3, vn3, cache_k4, cache_v4,
                *, lam_init):
    m = x2d.shape[0]
    n, n_pages = page_table.shape
    tm = m // n
    assert tm * n == m and tm % SUBLANES == 0
    pt_flat = page_table.reshape(-1)
    const2 = lambda i, pt: (0, 0)
    row = pl.BlockSpec((tm, D_MODEL), lambda i, pt: (i, 0))
    seq = pl.BlockSpec((None, N_HEADS, HEAD_W), lambda i, pt: (i, 0, 0))

    def whole(shape):
        return pl.BlockSpec(shape, const2, pipeline_mode=pl.Buffered(1))

    def page_spec(j):
        return pl.BlockSpec((None, PAGE_SIZE, N_HEADS, HEAD_W), lambda i, pt: (pt[i * n_pages + j], 0, 0, 0))

    grid_spec = pltpu.PrefetchScalarGridSpec(
        num_scalar_prefetch=1,
        grid=(n,),
        in_specs=[row, whole((1, D_MODEL)), whole(wup_b.shape), whole(wdn_b.shape), whole((1, D_MODEL)),
                  whole((SUBLANES, LANES)), whole((1, HEAD_W)), seq, seq, seq]
                 + [page_spec(j) for j in range(n_pages)] * 2,
        out_specs=(row, seq),
        scratch_shapes=[pltpu.VMEM((tm, D_MODEL), BF16)],
    )
    kern = functools.partial(_mlp_decode_kernel, n_pages=n_pages, lam_init=lam_init)
    return pl.pallas_call(
        kern,
        grid_spec=grid_spec,
        out_shape=(jax.ShapeDtypeStruct((m, D_MODEL), F32), jax.ShapeDtypeStruct((n, N_HEADS, HEAD_W), F32)),
        compiler_params=_params("parallel"),
        name="mlp_decode",
    )(pt_flat, x2d, g_mlp, wup_b, wdn_b, g_final, lam, g_subln, q3, kn3, vn3,
      *([cache_k4] * n_pages), *([cache_v4] * n_pages))


PROMPT_TM = 512
MIX_TM = 1024
ATTN_BLK = 512


def kernel(x_prompt, x_sample, state_pool, cache_k, cache_v, page_table, g_mix, w_in, w_pool_grp, pool_scale,
           w_pool_proj, lambda_q1, lambda_k1, lambda_q2, lambda_k2, g_subln, w_attn_proj, w_o, g_mlp, w_up,
           w_down, g_final):
    depth = w_in.shape[0]
    assert depth == 1, "single-layer trunk"
    layer = 0
    lam_init = _lambda_init(layer)
    b, t, _ = x_prompt.shape
    n, t_s, _ = x_sample.shape
    assert t_s == 1 and t % PROMPT_TM == 0 and t % ATTN_BLK == 0
    past_len = page_table.shape[1] * PAGE_SIZE

    w_in_b = w_in[layer].astype(BF16)
    wg_b = w_pool_grp[layer].astype(BF16)
    wpp_b = w_pool_proj[layer].astype(BF16)
    wap_b = w_attn_proj[layer].astype(BF16)
    wo_b = w_o[layer].astype(BF16)
    wup_b = w_up[layer].astype(BF16)
    wdn_b = w_down[layer].astype(BF16)
    g_mix2 = g_mix[layer][None, :]
    g_mlp2 = g_mlp[layer][None, :]
    g_fin2 = g_final[None, :]
    g_sub2 = g_subln[layer][None, :]
    scale2 = pool_scale[layer][None, :]

    cos_tab, sin_tab, lam = _prep(lambda_q1[layer], lambda_k1[layer], lambda_q2[layer], lambda_k2[layer],
                                  n_prompt=t, n_sample_rows=n, past_len=past_len, lam_init=lam_init)

    xs2 = x_sample.reshape(n, D_MODEL)
    u_s, q_s, k_s, v_s, _, _, gates_s = _proj(
        xs2, g_mix2, w_in_b, cos_tab, sin_tab, tm=n, tab_block_of=lambda i: t // n)
    tiles_per_seq = t // PROMPT_TM
    xp2 = x_prompt.reshape(b * t, D_MODEL)
    u_p, q_p, k_p, v_p, kb_p, vb_p, gates_p, pooled_p = _proj(
        xp2, g_mix2, w_in_b, cos_tab, sin_tab, tm=PROMPT_TM, tab_block_of=lambda i: i % tiles_per_seq,
        tiles_per_seq=tiles_per_seq)

    on_p = _attn_prompt(lam, g_sub2, q_p.reshape(b, t, D_ATTN), kb_p.reshape(b, t, D_ATTN),
                        vb_p.reshape(b, t, D_ATTN), blk=ATTN_BLK, lam_init=lam_init)
    x1_p = _mix(xp2, pooled_p, on_p.reshape(b * t, D_ATTN), gates_p, wg_b, scale2, wpp_b, wap_b, wo_b,
                tm=MIX_TM)

    heads = lambda a: a.reshape(n, N_HEADS, HEAD_W)
    y_p, on_s = _mlp_decode(x1_p, g_mlp2, wup_b, wdn_b, g_fin2, page_table, lam, g_sub2,
                            heads(q_s), heads(k_s), heads(v_s), cache_k[layer], cache_v[layer], lam_init=lam_init)

    state_t = jnp.swapaxes(state_pool[layer], 0, 1)
    pooled_s = _pool_sample(state_t, u_s, past_len=past_len)
    x1_s = _mix(xs2, pooled_s, on_s.reshape(n, D_ATTN), gates_s, wg_b, scale2, wpp_b, wap_b, wo_b, tm=n)
    y_s = _mlp(x1_s, g_mlp2, wup_b, wdn_b, g_fin2, tm=n)

    y_prompt = y_p.reshape(b, t, D_MODEL)
    y_sample = y_s.reshape(n, 1, D_MODEL)
    new_k_prompt = k_p.reshape(1, b, t, N_HEADS, HEAD_W)
    new_v_prompt = v_p.reshape(1, b, t, N_HEADS, HEAD_W)
    new_pool_prompt = u_p.reshape(b, t, D_POOL)[:, t - POOL_BUF:, :][None]
    new_k_sample = k_s.reshape(1, n, 1, N_HEADS, HEAD_W)
    new_v_sample = v_s.reshape(1, n, 1, N_HEADS, HEAD_W)
    new_pool_sample = jnp.concatenate([state_pool[layer][:, 1:, :], u_s[:, None, :]], axis=1)[None]
    return (y_prompt, y_sample, new_k_prompt, new_v_prompt, new_pool_prompt,
            new_k_sample, new_v_sample, new_pool_sample)
```

```python
import functools
import math

import jax
import jax.numpy as jnp
from jax import lax
from jax.experimental import pallas as pl
from jax.experimental.pallas import tpu as pltpu

D_MODEL = 1024
N_HEADS = 8
HEAD_DIM = 64
HEAD_W = 2 * HEAD_DIM
D_ATTN = N_HEADS * HEAD_W
POOL_WINDOWS = (2, 4, 8, 16)
D_POOL = D_MODEL // 2
POOL_GROUP = D_POOL // len(POOL_WINDOWS)
POOL_BUF = max(POOL_WINDOWS) - 1
HALO = 16
D_FF = 4 * D_MODEL
ROPE_THETA = 10000.0
RMS_EPS = 1e-6
SUBLN_EPS = 1e-5
NEG_INF = -1e30
PAGE_SIZE = 128
OFF_Q = D_POOL
OFF_K = OFF_Q + D_ATTN
OFF_V = OFF_K + D_ATTN
OFF_G = OFF_V + D_ATTN
D_IN = OFF_G + 2 * D_MODEL

LANES = 128
SUBLANES = 8
VMEM_LIMIT = 56 * 1024 * 1024

BF16 = jnp.bfloat16
F32 = jnp.float32


def _lambda_init(layer):
    return 0.8 - 0.6 * math.exp(-0.3 * layer)


def _resident(shape):
    nd = len(shape)
    return pl.BlockSpec(shape, lambda *_: (0,) * nd, pipeline_mode=pl.Buffered(1))


def _slab(block, axis, c):
    index = (lambda *_: (c, 0)) if axis == 0 else (lambda *_: (0, c))
    return pl.BlockSpec(block, index, pipeline_mode=pl.Buffered(1))


def _params(*sem):
    return pltpu.CompilerParams(dimension_semantics=sem, vmem_limit_bytes=VMEM_LIMIT)


def _prep_kernel(invf_ref, lq1_ref, lk1_ref, lq2_ref, lk2_ref, cos_ref, sin_ref, lam_ref, *, n_prompt, past_len, lam_init):
    rows = cos_ref.shape[0]
    row = lax.broadcasted_iota(jnp.int32, (rows, LANES), 0)
    lane = lax.broadcasted_iota(jnp.int32, (rows, LANES), 1)
    pos = jnp.where(row < n_prompt, row, past_len).astype(F32)
    ang = pos * invf_ref[...]
    cos_ref[...] = jnp.cos(ang)
    sin_ref[...] = jnp.where((lane % HEAD_DIM) < HEAD_DIM // 2, -jnp.sin(ang), jnp.sin(ang))
    d1 = jnp.sum(lq1_ref[...] * lk1_ref[...], axis=-1, keepdims=True)
    d2 = jnp.sum(lq2_ref[...] * lk2_ref[...], axis=-1, keepdims=True)
    lam = jnp.exp(d1) - jnp.exp(d2) + lam_init
    lam_ref[...] = jnp.broadcast_to(lam, lam_ref.shape)


def _prep(lq1, lk1, lq2, lk2, *, n_prompt, n_sample_rows, past_len, lam_init):
    half = HEAD_DIM // 2
    inv_freq = ROPE_THETA ** (-jnp.arange(half, dtype=F32) * (2.0 / HEAD_DIM))
    invf = jnp.tile(inv_freq, LANES // half)[None, :]
    rows = n_prompt + n_sample_rows
    kern = functools.partial(_prep_kernel, n_prompt=n_prompt, past_len=past_len, lam_init=lam_init)
    return pl.pallas_call(
        kern,
        out_shape=(jax.ShapeDtypeStruct((rows, LANES), F32),
                   jax.ShapeDtypeStruct((rows, LANES), F32),
                   jax.ShapeDtypeStruct((SUBLANES, LANES), F32)),
        name="prep",
    )(invf, lq1[None, :], lk1[None, :], lq2[None, :], lk2[None, :])


PROJ_CHUNK = 512
Q_SCALE = HEAD_DIM ** -0.5 * math.log2(math.e)
N_W_CHUNKS = D_IN // PROJ_CHUNK


def _rope(z, cos, sin, first_half):
    swapped = jnp.where(first_half, pltpu.roll(z, LANES - HEAD_DIM // 2, 1), pltpu.roll(z, HEAD_DIM // 2, 1))
    return z * cos + swapped * sin


POOL_ROWS = 64


def _window_means(ext_scr, pos, o_ref, tm):
    for g, w in enumerate(POOL_WINDOWS):
        sl = slice(g * POOL_GROUP, (g + 1) * POOL_GROUP)
        for r0 in range(0, tm, POOL_ROWS):
            cur = ext_scr[HALO + r0:HALO + r0 + POOL_ROWS, sl]
            acc = cur
            for k in range(1, w):
                acc = acc + ext_scr[HALO + r0 - k:HALO + r0 - k + POOL_ROWS, sl]
            count = jnp.minimum(pos[r0:r0 + POOL_ROWS] + 1, w).astype(F32)
            o_ref[r0:r0 + POOL_ROWS, sl] = (acc / count - cur).astype(o_ref.dtype)


def _proj_kernel(x_ref, g_ref, *refs, tiles_per_seq):
    w_refs = refs[:N_W_CHUNKS]
    cos_ref, sin_ref, u_ref, q_ref, k_ref, v_ref, kb_ref, vb_ref, gate_ref = refs[N_W_CHUNKS:N_W_CHUNKS + 9]
    rest = refs[N_W_CHUNKS + 9:]
    if tiles_per_seq is None:
        (h_scr,) = rest
    else:
        pooled_ref, h_scr, ext_scr = rest
    tm = x_ref.shape[0]
    if tiles_per_seq is not None:
        i = pl.program_id(0) % tiles_per_seq

        @pl.when(i == 0)
        def _():
            ext_scr[0:HALO, :] = jnp.zeros((HALO, D_POOL), F32)

        @pl.when(i > 0)
        def _():
            ext_scr[0:HALO, :] = ext_scr[tm:tm + HALO, :]

    x = x_ref[...]
    ms = jnp.mean(x * x, axis=-1, keepdims=True)
    h_scr[...] = (x * lax.rsqrt(ms + RMS_EPS) * g_ref[...]).astype(BF16)
    cos = cos_ref[...]
    sin = sin_ref[...]
    lane = lax.broadcasted_iota(jnp.int32, (tm, LANES), 1)
    first_half = (lane % HEAD_DIM) < HEAD_DIM // 2

    def dot(off, width):
        assert off % PROJ_CHUNK == 0 and width == PROJ_CHUNK
        return jnp.dot(h_scr[...], w_refs[off // PROJ_CHUNK][...], preferred_element_type=F32)

    u = dot(0, D_POOL)
    u_ref[...] = u
    if tiles_per_seq is not None:
        ext_scr[HALO:HALO + tm, :] = u
    for c in range(D_ATTN // PROJ_CHUNK):
        zq = dot(OFF_Q + c * PROJ_CHUNK, PROJ_CHUNK)
        zk = dot(OFF_K + c * PROJ_CHUNK, PROJ_CHUNK)
        for j in range(PROJ_CHUNK // LANES):
            sl = slice(j * LANES, (j + 1) * LANES)
            osl = slice(c * PROJ_CHUNK + j * LANES, c * PROJ_CHUNK + (j + 1) * LANES)
            q_ref[:, osl] = (_rope(zq[:, sl], cos, sin, first_half) * Q_SCALE).astype(BF16)
            kr = _rope(zk[:, sl], cos, sin, first_half)
            k_ref[:, osl] = kr
            kb_ref[:, osl] = kr.astype(BF16)
        zv = dot(OFF_V + c * PROJ_CHUNK, PROJ_CHUNK)
        csl = slice(c * PROJ_CHUNK, (c + 1) * PROJ_CHUNK)
        v_ref[:, csl] = zv
        vb_ref[:, csl] = zv.astype(BF16)
    if tiles_per_seq is not None:
        _window_means(ext_scr, i * tm + lax.broadcasted_iota(jnp.int32, (tm, POOL_GROUP), 0), pooled_ref, tm)
    for c in range(2 * D_MODEL // PROJ_CHUNK):
        zg = dot(OFF_G + c * PROJ_CHUNK, PROJ_CHUNK)
        gate_ref[:, c * PROJ_CHUNK:(c + 1) * PROJ_CHUNK] = jax.nn.sigmoid(zg)


def _proj(x2d, g_mix, w_in_b, cos_tab, sin_tab, *, tm, tab_block_of, tiles_per_seq=None):
    m = x2d.shape[0]
    row = lambda i: (i, 0)
    tab = pl.BlockSpec((tm, LANES), lambda i: (tab_block_of(i), 0))
    outs = [
        jax.ShapeDtypeStruct((m, D_POOL), F32),
        jax.ShapeDtypeStruct((m, D_ATTN), BF16),
        jax.ShapeDtypeStruct((m, D_ATTN), F32),
        jax.ShapeDtypeStruct((m, D_ATTN), F32),
        jax.ShapeDtypeStruct((m, D_ATTN), BF16),
        jax.ShapeDtypeStruct((m, D_ATTN), BF16),
        jax.ShapeDtypeStruct((m, 2 * D_MODEL), F32),
    ]
    scratch = [pltpu.VMEM((tm, D_MODEL), BF16)]
    if tiles_per_seq is not None:
        outs.append(jax.ShapeDtypeStruct((m, D_POOL), BF16))
        scratch.append(pltpu.VMEM((HALO + tm, D_POOL), F32))
    return pl.pallas_call(
        functools.partial(_proj_kernel, tiles_per_seq=tiles_per_seq),
        grid=(m // tm,),
        in_specs=[pl.BlockSpec((tm, D_MODEL), row), _resident((1, D_MODEL))]
                 + [_slab((D_MODEL, PROJ_CHUNK), 1, c) for c in range(N_W_CHUNKS)] + [tab, tab],
        out_specs=tuple(pl.BlockSpec((tm, s.shape[1]), row) for s in outs),
        out_shape=tuple(outs),
        scratch_shapes=scratch,
        compiler_params=_params("parallel" if tiles_per_seq is None else "arbitrary"),
        name="proj",
    )(x2d, g_mix, *([w_in_b] * N_W_CHUNKS), cos_tab, sin_tab)


def _pool_sample_kernel(state_ref, u_ref, o_ref, *, past_len):
    for g, w in enumerate(POOL_WINDOWS):
        sl = slice(g * POOL_GROUP, (g + 1) * POOL_GROUP)
        cur = u_ref[:, sl]
        acc = cur
        for k in range(1, w):
            acc = acc + state_ref[POOL_BUF - k, :, sl]
        count = float(min(past_len + 1, w))
        o_ref[:, sl] = (acc / count - cur).astype(o_ref.dtype)


def _pool_sample(state_t, u_s, *, past_len):
    n = u_s.shape[0]
    return pl.pallas_call(
        functools.partial(_pool_sample_kernel, past_len=past_len),
        out_shape=jax.ShapeDtypeStruct((n, D_POOL), BF16),
        name="pool_sample",
    )(state_t, u_s)


def _subln(o, g, lam_init):
    ms = jnp.mean(o * o, axis=-1, keepdims=True)
    return o * lax.rsqrt(ms + SUBLN_EPS) * g * (1.0 - lam_init)


ONES_ROWS = 16
VT_ROWS = HEAD_W + ONES_ROWS
COL_STRIP = 256


def _attn_prompt_kernel(lam_ref, g_ref, q_ref, k_ref, v_ref, o_ref,
                        vt_scr, qt_scr, s0_scr, s1_scr, m_scr, acc_scr, *, blk, lam_init):
    nblk = k_ref.shape[0] // blk
    drow = lax.broadcasted_iota(jnp.int32, (HEAD_W, blk), 0)
    for r in range(nblk):
        rows = slice(r * blk, (r + 1) * blk)
        vt_scr[r, 0:HEAD_W, :] = v_ref[rows, :].astype(F32).T.astype(BF16)
        vt_scr[r, HEAD_W:VT_ROWS, :] = jnp.ones((ONES_ROWS, blk), BF16)
        qt = q_ref[rows, :].astype(F32).T
        qt_scr[r, 0] = jnp.where(drow < HEAD_DIM, qt, 0.0).astype(BF16)
        qt_scr[r, 1] = jnp.where(drow >= HEAD_DIM, qt, 0.0).astype(BF16)
    lam = lam_ref[0:1, 0:1]
    gain = g_ref[...] * (1.0 - lam_init)
    s_bufs = (s0_scr, s1_scr)
    units = [(qi, j) for qi in range(nblk) for j in range(qi + 1)]
    strips = [slice(h * COL_STRIP, (h + 1) * COL_STRIP) for h in range(blk // COL_STRIP)]

    def n_keys(qi, j, h):
        return (h + 1) * COL_STRIP if j == qi else blk

    def fill(u):
        qi, j = units[u]
        for c in range(2):
            if j != qi:
                s_bufs[u % 2][c] = jnp.dot(k_ref[j * blk:(j + 1) * blk, :], qt_scr[qi, c],
                                           preferred_element_type=F32)
                continue
            for h, cols in enumerate(strips):
                nk = n_keys(qi, j, h)
                s = jnp.dot(k_ref[j * blk:j * blk + nk, :], qt_scr[qi, c, :, cols], preferred_element_type=F32)
                krow = lax.broadcasted_iota(jnp.int32, (nk, COL_STRIP), 0)
                qcol = lax.broadcasted_iota(jnp.int32, (nk, COL_STRIP), 1) + h * COL_STRIP
                s_bufs[u % 2][c, 0:nk, cols] = jnp.where(krow <= qcol, s, NEG_INF)

    def consume(u):
        qi, j = units[u]
        st = qi % 2
        for c in range(2):
            for h, cols in enumerate(strips):
                nk = n_keys(qi, j, h)
                vt = vt_scr[j, :, 0:nk]
                s_ref = s_bufs[u % 2].at[c, 0:nk, cols]
                m_blk = jnp.max(s_ref[...], axis=0, keepdims=True)
                if j == 0:
                    m_new = m_blk
                    p = jnp.exp2(s_ref[...] - m_new).astype(BF16)
                    acc_scr[st, c, :, cols] = jnp.dot(vt, p, preferred_element_type=F32)
                else:
                    m_prev = m_scr[st, c, :, cols]
                    m_new = jnp.maximum(m_prev, m_blk)
                    p = jnp.exp2(s_ref[...] - m_new).astype(BF16)
                    acc_scr[st, c, :, cols] = (jnp.exp2(m_prev - m_new) * acc_scr[st, c, :, cols]
                                               + jnp.dot(vt, p, preferred_element_type=F32))
                m_scr[st, c, :, cols] = m_new

    def finalize(qi):
        a1 = acc_scr[qi % 2, 0]
        a2 = acc_scr[qi % 2, 1]
        ot = (a1[0:HEAD_W] * (1.0 / a1[HEAD_W:HEAD_W + 1])
              - lam * (a2[0:HEAD_W] * (1.0 / a2[HEAD_W:HEAD_W + 1])))
        ms = jnp.mean(ot * ot, axis=0, keepdims=True)
        on = (ot * lax.rsqrt(ms + SUBLN_EPS)).T * gain
        o_ref[qi * blk:(qi + 1) * blk, :] = on.astype(o_ref.dtype)

    fill(0)
    for u, (qi, j) in enumerate(units):
        if u + 1 < len(units):
            fill(u + 1)
        consume(u)
        if j == qi:
            finalize(qi)


def _attn_prompt(lam, g_subln, q, kb, vb, *, blk, lam_init):
    b, t, _ = q.shape
    kern = functools.partial(_attn_prompt_kernel, blk=blk, lam_init=lam_init)
    seq = pl.BlockSpec((None, t, HEAD_W), lambda bi, h: (bi, 0, h))
    return pl.pallas_call(
        kern,
        grid=(b, N_HEADS),
        in_specs=[_resident((SUBLANES, LANES)), _resident((1, HEAD_W)), seq, seq, seq],
        out_specs=seq,
        out_shape=jax.ShapeDtypeStruct((b, t, D_ATTN), BF16),
        scratch_shapes=[pltpu.VMEM((t // blk, VT_ROWS, blk), BF16), pltpu.VMEM((t // blk, 2, HEAD_W, blk), BF16),
                        pltpu.VMEM((2, blk, blk), F32), pltpu.VMEM((2, blk, blk), F32),
                        pltpu.VMEM((2, 2, 1, blk), F32), pltpu.VMEM((2, 2, VT_ROWS, blk), F32)],
        compiler_params=_params("parallel", "parallel"),
        name="attn_prompt",
    )(lam, g_subln, q, kb, vb)


N_MAPROWS = 2 * N_HEADS
HALF_TOKENS = PAGE_SIZE // 2
HALF_KEYS = HALF_TOKENS * N_HEADS
QUARTER_TOKENS = PAGE_SIZE // 4
QUARTER_KEYS = QUARTER_TOKENS * N_HEADS


def _decode_weights(q_ref, kn_ref, k_refs):
    row = lax.broadcasted_iota(jnp.int32, (N_MAPROWS, HEAD_W), 0)
    lane = lax.broadcasted_iota(jnp.int32, (N_MAPROWS, HEAD_W), 1)
    q = q_ref[...].astype(F32)
    qm = jnp.where((lane >= HEAD_DIM) == (row >= N_HEADS), jnp.concatenate([q, q], axis=0), 0.0)
    zero = jnp.zeros_like(qm)
    qm2 = jnp.concatenate([jnp.concatenate([qm, zero], axis=1),
                           jnp.concatenate([zero, qm], axis=1)], axis=0).astype(BF16)

    krow = lax.broadcasted_iota(jnp.int32, (2 * N_MAPROWS, HALF_KEYS), 0)
    kcol = lax.broadcasted_iota(jnp.int32, (2 * N_MAPROWS, HALF_KEYS), 1)
    head_bits = N_HEADS - 1
    own = (kcol & head_bits) == (krow & head_bits)

    def scores(queries, keys):
        return lax.dot_general(queries, keys, (((1,), (1,)), ((), ())), preferred_element_type=F32)

    s_pages = [jnp.where(own, scores(qm2, _page_halves(kr)), NEG_INF) for kr in k_refs]
    s_new = jnp.where(lane == (row & head_bits), scores(qm.astype(BF16), _pad_new(kn_ref)), NEG_INF)

    m2 = jnp.max(s_pages[0], axis=-1, keepdims=True)
    for s in s_pages[1:]:
        m2 = jnp.maximum(m2, jnp.max(s, axis=-1, keepdims=True))
    m = jnp.maximum(jnp.maximum(m2[0:N_MAPROWS], m2[N_MAPROWS:2 * N_MAPROWS]), jnp.max(s_new, axis=-1, keepdims=True))
    mm = jnp.concatenate([m, m], axis=0)
    p_new = jnp.exp2(s_new - m)
    l = jnp.sum(p_new, axis=-1, keepdims=True)
    l2 = jnp.zeros((2 * N_MAPROWS, 1), F32)
    p_pages = []
    for s in s_pages:
        p = jnp.exp2(s - mm)
        l2 = l2 + jnp.sum(p, axis=-1, keepdims=True)
        p_pages.append(p.astype(BF16))
    l = l + l2[0:N_MAPROWS] + l2[N_MAPROWS:2 * N_MAPROWS]
    return p_pages, p_new.astype(BF16), l


def _page_halves(ref):
    a = ref[0:HALF_TOKENS].reshape(HALF_KEYS, HEAD_W)
    b = ref[HALF_TOKENS:PAGE_SIZE].reshape(HALF_KEYS, HEAD_W)
    return jnp.concatenate([a, b], axis=1).astype(BF16)


def _pad_new(ref):
    pad = jnp.zeros((LANES - N_HEADS, HEAD_W), F32)
    return jnp.concatenate([ref[...], pad], axis=0).astype(BF16)


def _page_quarters(ref):
    parts = [ref[i * QUARTER_TOKENS:(i + 1) * QUARTER_TOKENS].reshape(QUARTER_KEYS, HEAD_W) for i in range(4)]
    return jnp.concatenate(parts, axis=1).astype(BF16)


def _decode_output(weights, lam_ref, g_ref, vn_ref, v_refs, o_ref, *, lam_init):
    p_pages, p_new, l = weights
    acc = jnp.dot(p_new, _pad_new(vn_ref), preferred_element_type=F32)
    acc4 = jnp.zeros((4 * N_MAPROWS, 4 * HEAD_W), F32)
    for p, vr in zip(p_pages, v_refs):
        p4 = jnp.concatenate([p[0:N_MAPROWS, 0:QUARTER_KEYS], p[0:N_MAPROWS, QUARTER_KEYS:HALF_KEYS],
                              p[N_MAPROWS:2 * N_MAPROWS, 0:QUARTER_KEYS],
                              p[N_MAPROWS:2 * N_MAPROWS, QUARTER_KEYS:HALF_KEYS]], axis=0)
        acc4 = acc4 + jnp.dot(p4, _page_quarters(vr), preferred_element_type=F32)
    for i in range(4):
        acc = acc + acc4[i * N_MAPROWS:(i + 1) * N_MAPROWS, i * HEAD_W:(i + 1) * HEAD_W]
    a = acc / l
    o = a[0:N_HEADS] - lam_ref[0:1, 0:1] * a[N_HEADS:N_MAPROWS]
    o_ref[...] = _subln(o, g_ref[...], lam_init).astype(o_ref.dtype)


def _mix_kernel(x_ref, pooled_ref, on_ref, gate_ref, wg_ref, scale_ref, wpp_ref, wap_ref, wo_ref, o_ref):
    parts = []
    for g in range(len(POOL_WINDOWS)):
        sl = slice(g * POOL_GROUP, (g + 1) * POOL_GROUP)
        parts.append(jnp.dot(pooled_ref[:, sl], wg_ref[g], preferred_element_type=F32))
    pool_out = jnp.concatenate(parts, axis=1) * scale_ref[...]
    pool_proj = jnp.dot(pool_out.astype(BF16), wpp_ref[...], preferred_element_type=F32)
    attn_proj = jnp.dot(on_ref[...].astype(BF16), wap_ref[...], preferred_element_type=F32)
    merged = gate_ref[:, 0:D_MODEL] * pool_proj + gate_ref[:, D_MODEL:2 * D_MODEL] * attn_proj
    o_ref[...] = x_ref[...] + jnp.dot(merged.astype(BF16), wo_ref[...], preferred_element_type=F32)


def _mix(x2d, pooled, on, gates, wg_b, pool_scale, wpp_b, wap_b, wo_b, *, tm):
    m = x2d.shape[0]
    row = lambda i: (i, 0)
    return pl.pallas_call(
        _mix_kernel,
        grid=(m // tm,),
        in_specs=[pl.BlockSpec((tm, D_MODEL), row), pl.BlockSpec((tm, D_POOL), row),
                  pl.BlockSpec((tm, D_ATTN), row), pl.BlockSpec((tm, 2 * D_MODEL), row),
                  _resident(wg_b.shape), _resident((1, D_POOL)), _resident(wpp_b.shape),
                  _resident(wap_b.shape), _resident(wo_b.shape)],
        out_specs=pl.BlockSpec((tm, D_MODEL), row),
        out_shape=jax.ShapeDtypeStruct((m, D_MODEL), F32),
        compiler_params=_params("parallel"),
        name="mix",
    )(x2d, pooled, on, gates, wg_b, pool_scale, wpp_b, wap_b, wo_b)


MLP_CHUNK = 1024


def _rms(x, g):
    ms = jnp.mean(x * x, axis=-1, keepdims=True)
    return x * lax.rsqrt(ms + RMS_EPS) * g


def _mlp_kernel(x_ref, g_ref, wup_ref, wdn_ref, gf_ref, o_ref, h_scr):
    x = x_ref[...]
    h_scr[...] = _rms(x, g_ref[...]).astype(BF16)
    y = x
    for c in range(D_FF // MLP_CHUNK):
        sl = slice(c * MLP_CHUNK, (c + 1) * MLP_CHUNK)
        a = jnp.maximum(jnp.dot(h_scr[...], wup_ref[:, sl], preferred_element_type=F32), 0.0)
        y = y + jnp.dot((a * a).astype(BF16), wdn_ref[sl, :], preferred_element_type=F32)
    o_ref[...] = _rms(y, gf_ref[...])


def _mlp(x2d, g_mlp, wup_b, wdn_b, g_final, *, tm):
    m = x2d.shape[0]
    row = lambda i: (i, 0)
    return pl.pallas_call(
        _mlp_kernel,
        grid=(m // tm,),
        in_specs=[pl.BlockSpec((tm, D_MODEL), row), _resident((1, D_MODEL)), _resident(wup_b.shape),
                  _resident(wdn_b.shape), _resident((1, D_MODEL))],
        out_specs=pl.BlockSpec((tm, D_MODEL), row),
        out_shape=jax.ShapeDtypeStruct((m, D_MODEL), F32),
        scratch_shapes=[pltpu.VMEM((tm, D_MODEL), BF16)],
        compiler_params=_params("parallel"),
        name="mlp",
    )(x2d, g_mlp, wup_b, wdn_b, g_final)


N_MLP_IN = 5
N_DEC_IN = 5


def _mlp_decode_kernel(pt_ref, *refs, n_pages, lam_init):
    del pt_ref
    mlp_in = refs[:N_MLP_IN]
    lam_ref, gs_ref, q_ref, kn_ref, vn_ref = refs[N_MLP_IN:N_MLP_IN + N_DEC_IN]
    pages = refs[N_MLP_IN + N_DEC_IN:N_MLP_IN + N_DEC_IN + 2 * n_pages]
    y_ref, on_ref, h_scr = refs[N_MLP_IN + N_DEC_IN + 2 * n_pages:]
    weights = _decode_weights(q_ref, kn_ref, pages[:n_pages])
    _mlp_kernel(*mlp_in, y_ref, h_scr)
    _decode_output(weights, lam_ref, gs_ref, vn_ref, pages[n_pages:], on_ref, lam_init=lam_init)


def _mlp_decode(x2d, g_mlp, wup_b, wdn_b, g_final, page_table, lam, g_subln, q3, kn3, vn3, cache_k4, cache_v4,
                *, lam_init):
    m = x2d.shape[0]
    n, n_pages = page_table.shape
    tm = m // n
    assert tm * n == m and tm % SUBLANES == 0
    pt_flat = page_table.reshape(-1)
    const2 = lambda i, pt: (0, 0)
    row = pl.BlockSpec((tm, D_MODEL), lambda i, pt: (i, 0))
    seq = pl.BlockSpec((None, N_HEADS, HEAD_W), lambda i, pt: (i, 0, 0))

    def whole(shape):
        return pl.BlockSpec(shape, const2, pipeline_mode=pl.Buffered(1))

    def page_spec(j):
        return pl.BlockSpec((None, PAGE_SIZE, N_HEADS, HEAD_W), lambda i, pt: (pt[i * n_pages + j], 0, 0, 0))

    grid_spec = pltpu.PrefetchScalarGridSpec(
        num_scalar_prefetch=1,
        grid=(n,),
        in_specs=[row, whole((1, D_MODEL)), whole(wup_b.shape), whole(wdn_b.shape), whole((1, D_MODEL)),
                  whole((SUBLANES, LANES)), whole((1, HEAD_W)), seq, seq, seq]
                 + [page_spec(j) for j in range(n_pages)] * 2,
        out_specs=(row, seq),
        scratch_shapes=[pltpu.VMEM((tm, D_MODEL), BF16)],
    )
    kern = functools.partial(_mlp_decode_kernel, n_pages=n_pages, lam_init=lam_init)
    return pl.pallas_call(
        kern,
        grid_spec=grid_spec,
        out_shape=(jax.ShapeDtypeStruct((m, D_MODEL), F32), jax.ShapeDtypeStruct((n, N_HEADS, HEAD_W), F32)),
        compiler_params=_params("parallel"),
        name="mlp_decode",
    )(pt_flat, x2d, g_mlp, wup_b, wdn_b, g_final, lam, g_subln, q3, kn3, vn3,
      *([cache_k4] * n_pages), *([cache_v4] * n_pages))


PROMPT_TM = 512
MIX_TM = 1024
ATTN_BLK = 512


def kernel(x_prompt, x_sample, state_pool, cache_k, cache_v, page_table, g_mix, w_in, w_pool_grp, pool_scale,
           w_pool_proj, lambda_q1, lambda_k1, lambda_q2, lambda_k2, g_subln, w_attn_proj, w_o, g_mlp, w_up,
           w_down, g_final):
    depth = w_in.shape[0]
    assert depth == 1, "single-layer trunk"
    layer = 0
    lam_init = _lambda_init(layer)
    b, t, _ = x_prompt.shape
    n, t_s, _ = x_sample.shape
    assert t_s == 1 and t % PROMPT_TM == 0 and t % ATTN_BLK == 0
    past_len = page_table.shape[1] * PAGE_SIZE

    w_in_b = w_in[layer].astype(BF16)
    wg_b = w_pool_grp[layer].astype(BF16)
    wpp_b = w_pool_proj[layer].astype(BF16)
    wap_b = w_attn_proj[layer].astype(BF16)
    wo_b = w_o[layer].astype(BF16)
    wup_b = w_up[layer].astype(BF16)
    wdn_b = w_down[layer].astype(BF16)
    g_mix2 = g_mix[layer][None, :]
    g_mlp2 = g_mlp[layer][None, :]
    g_fin2 = g_final[None, :]
    g_sub2 = g_subln[layer][None, :]
    scale2 = pool_scale[layer][None, :]

    cos_tab, sin_tab, lam = _prep(lambda_q1[layer], lambda_k1[layer], lambda_q2[layer], lambda_k2[layer],
                                  n_prompt=t, n_sample_rows=n, past_len=past_len, lam_init=lam_init)

    xs2 = x_sample.reshape(n, D_MODEL)
    u_s, q_s, k_s, v_s, _, _, gates_s = _proj(
        xs2, g_mix2, w_in_b, cos_tab, sin_tab, tm=n, tab_block_of=lambda i: t // n)
    tiles_per_seq = t // PROMPT_TM
    xp2 = x_prompt.reshape(b * t, D_MODEL)
    u_p, q_p, k_p, v_p, kb_p, vb_p, gates_p, pooled_p = _proj(
        xp2, g_mix2, w_in_b, cos_tab, sin_tab, tm=PROMPT_TM, tab_block_of=lambda i: i % tiles_per_seq,
        tiles_per_seq=tiles_per_seq)

    on_p = _attn_prompt(lam, g_sub2, q_p.reshape(b, t, D_ATTN), kb_p.reshape(b, t, D_ATTN),
                        vb_p.reshape(b, t, D_ATTN), blk=ATTN_BLK, lam_init=lam_init)
    x1_p = _mix(xp2, pooled_p, on_p.reshape(b * t, D_ATTN), gates_p, wg_b, scale2, wpp_b, wap_b, wo_b,
                tm=MIX_TM)

    heads = lambda a: a.reshape(n, N_HEADS, HEAD_W)
    y_p, on_s = _mlp_decode(x1_p, g_mlp2, wup_b, wdn_b, g_fin2, page_table, lam, g_sub2,
                            heads(q_s), heads(k_s), heads(v_s), cache_k[layer], cache_v[layer], lam_init=lam_init)

    state_t = jnp.swapaxes(state_pool[layer], 0, 1)
    pooled_s = _pool_sample(state_t, u_s, past_len=past_len)
    x1_s = _mix(xs2, pooled_s, on_s.reshape(n, D_ATTN), gates_s, wg_b, scale2, wpp_b, wap_b, wo_b, tm=n)
    y_s = _mlp(x1_s, g_mlp2, wup_b, wdn_b, g_fin2, tm=n)

    y_prompt = y_p.reshape(b, t, D_MODEL)
    y_sample = y_s.reshape(n, 1, D_MODEL)
    new_k_prompt = k_p.reshape(1, b, t, N_HEADS, HEAD_W)
    new_v_prompt = v_p.reshape(1, b, t, N_HEADS, HEAD_W)
    new_pool_prompt = u_p.reshape(b, t, D_POOL)[:, t - POOL_BUF:, :][None]
    new_k_sample = k_s.reshape(1, n, 1, N_HEADS, HEAD_W)
    new_v_sample = v_s.reshape(1, n, 1, N_HEADS, HEAD_W)
    new_pool_sample = jnp.concatenate([state_pool[layer][:, 1:, :], u_s[:, None, :]], axis=1)[None]
    return (y_prompt, y_sample, new_k_prompt, new_v_prompt, new_pool_prompt,
            new_k_sample, new_v_sample, new_pool_sample)
```

```python
import functools
import math

import jax
import jax.numpy as jnp
from jax import lax
from jax.experimental import pallas as pl
from jax.experimental.pallas import tpu as pltpu

D_MODEL = 1024
N_HEADS = 8
HEAD_DIM = 64
HEAD_W = 2 * HEAD_DIM
D_ATTN = N_HEADS * HEAD_W
POOL_WINDOWS = (2, 4, 8, 16)
D_POOL = D_MODEL // 2
POOL_GROUP = D_POOL // len(POOL_WINDOWS)
POOL_BUF = max(POOL_WINDOWS) - 1
HALO = 16
D_FF = 4 * D_MODEL
ROPE_THETA = 10000.0
RMS_EPS = 1e-6
SUBLN_EPS = 1e-5
NEG_INF = -1e30
PAGE_SIZE = 128
OFF_Q = D_POOL
OFF_K = OFF_Q + D_ATTN
OFF_V = OFF_K + D_ATTN
OFF_G = OFF_V + D_ATTN
D_IN = OFF_G + 2 * D_MODEL

LANES = 128
SUBLANES = 8
VMEM_LIMIT = 56 * 1024 * 1024

BF16 = jnp.bfloat16
F32 = jnp.float32


def _lambda_init(layer):
    return 0.8 - 0.6 * math.exp(-0.3 * layer)


def _resident(shape):
    nd = len(shape)
    return pl.BlockSpec(shape, lambda *_: (0,) * nd, pipeline_mode=pl.Buffered(1))


def _params(*sem):
    return pltpu.CompilerParams(dimension_semantics=sem, vmem_limit_bytes=VMEM_LIMIT)


def _prep_kernel(invf_ref, lq1_ref, lk1_ref, lq2_ref, lk2_ref, cos_ref, sin_ref, lam_ref, *, n_prompt, past_len, lam_init):
    rows = cos_ref.shape[0]
    row = lax.broadcasted_iota(jnp.int32, (rows, LANES), 0)
    lane = lax.broadcasted_iota(jnp.int32, (rows, LANES), 1)
    pos = jnp.where(row < n_prompt, row, past_len).astype(F32)
    ang = pos * invf_ref[...]
    cos_ref[...] = jnp.cos(ang)
    sin_ref[...] = jnp.where((lane % HEAD_DIM) < HEAD_DIM // 2, -jnp.sin(ang), jnp.sin(ang))
    d1 = jnp.sum(lq1_ref[...] * lk1_ref[...], axis=-1, keepdims=True)
    d2 = jnp.sum(lq2_ref[...] * lk2_ref[...], axis=-1, keepdims=True)
    lam = jnp.exp(d1) - jnp.exp(d2) + lam_init
    lam_ref[...] = jnp.broadcast_to(lam, lam_ref.shape)


def _prep(lq1, lk1, lq2, lk2, *, n_prompt, n_sample_rows, past_len, lam_init):
    half = HEAD_DIM // 2
    inv_freq = ROPE_THETA ** (-jnp.arange(half, dtype=F32) * (2.0 / HEAD_DIM))
    invf = jnp.tile(inv_freq, LANES // half)[None, :]
    rows = n_prompt + n_sample_rows
    kern = functools.partial(_prep_kernel, n_prompt=n_prompt, past_len=past_len, lam_init=lam_init)
    return pl.pallas_call(
        kern,
        out_shape=(jax.ShapeDtypeStruct((rows, LANES), F32),
                   jax.ShapeDtypeStruct((rows, LANES), F32),
                   jax.ShapeDtypeStruct((SUBLANES, LANES), F32)),
        name="prep",
    )(invf, lq1[None, :], lk1[None, :], lq2[None, :], lk2[None, :])


PROJ_CHUNK = 512
Q_SCALE = HEAD_DIM ** -0.5 * math.log2(math.e)


def _rope(z, cos, sin, first_half):
    swapped = jnp.where(first_half, pltpu.roll(z, LANES - HEAD_DIM // 2, 1), pltpu.roll(z, HEAD_DIM // 2, 1))
    return z * cos + swapped * sin


POOL_ROWS = 64


def _window_means(ext_scr, pos, o_ref, tm):
    for g, w in enumerate(POOL_WINDOWS):
        sl = slice(g * POOL_GROUP, (g + 1) * POOL_GROUP)
        for r0 in range(0, tm, POOL_ROWS):
            cur = ext_scr[HALO + r0:HALO + r0 + POOL_ROWS, sl]
            acc = cur
            for k in range(1, w):
                acc = acc + ext_scr[HALO + r0 - k:HALO + r0 - k + POOL_ROWS, sl]
            count = jnp.minimum(pos[r0:r0 + POOL_ROWS] + 1, w).astype(F32)
            o_ref[r0:r0 + POOL_ROWS, sl] = (acc / count - cur).astype(o_ref.dtype)


def _proj_kernel(x_ref, g_ref, w_ref, cos_ref, sin_ref,
                 u_ref, q_ref, k_ref, v_ref, kb_ref, vb_ref, gate_ref, *rest, tiles_per_seq):
    if tiles_per_seq is None:
        (h_scr,) = rest
    else:
        pooled_ref, h_scr, ext_scr = rest
    tm = x_ref.shape[0]
    if tiles_per_seq is not None:
        i = pl.program_id(0) % tiles_per_seq

        @pl.when(i == 0)
        def _():
            ext_scr[0:HALO, :] = jnp.zeros((HALO, D_POOL), F32)

        @pl.when(i > 0)
        def _():
            ext_scr[0:HALO, :] = ext_scr[tm:tm + HALO, :]

    x = x_ref[...]
    ms = jnp.mean(x * x, axis=-1, keepdims=True)
    h_scr[...] = (x * lax.rsqrt(ms + RMS_EPS) * g_ref[...]).astype(BF16)
    cos = cos_ref[...]
    sin = sin_ref[...]
    lane = lax.broadcasted_iota(jnp.int32, (tm, LANES), 1)
    first_half = (lane % HEAD_DIM) < HEAD_DIM // 2

    def dot(off, width):
        return jnp.dot(h_scr[...], w_ref[:, off:off + width], preferred_element_type=F32)

    u = dot(0, D_POOL)
    u_ref[...] = u
    if tiles_per_seq is not None:
        ext_scr[HALO:HALO + tm, :] = u
    for c in range(D_ATTN // PROJ_CHUNK):
        zq = dot(OFF_Q + c * PROJ_CHUNK, PROJ_CHUNK)
        zk = dot(OFF_K + c * PROJ_CHUNK, PROJ_CHUNK)
        for j in range(PROJ_CHUNK // LANES):
            sl = slice(j * LANES, (j + 1) * LANES)
            osl = slice(c * PROJ_CHUNK + j * LANES, c * PROJ_CHUNK + (j + 1) * LANES)
            q_ref[:, osl] = (_rope(zq[:, sl], cos, sin, first_half) * Q_SCALE).astype(BF16)
            kr = _rope(zk[:, sl], cos, sin, first_half)
            k_ref[:, osl] = kr
            kb_ref[:, osl] = kr.astype(BF16)
        zv = dot(OFF_V + c * PROJ_CHUNK, PROJ_CHUNK)
        csl = slice(c * PROJ_CHUNK, (c + 1) * PROJ_CHUNK)
        v_ref[:, csl] = zv
        vb_ref[:, csl] = zv.astype(BF16)
    if tiles_per_seq is not None:
        _window_means(ext_scr, i * tm + lax.broadcasted_iota(jnp.int32, (tm, POOL_GROUP), 0), pooled_ref, tm)
    for c in range(2 * D_MODEL // PROJ_CHUNK):
        zg = dot(OFF_G + c * PROJ_CHUNK, PROJ_CHUNK)
        gate_ref[:, c * PROJ_CHUNK:(c + 1) * PROJ_CHUNK] = jax.nn.sigmoid(zg)


def _proj(x2d, g_mix, w_in_b, cos_tab, sin_tab, *, tm, tab_block_of, tiles_per_seq=None):
    m = x2d.shape[0]
    row = lambda i: (i, 0)
    tab = pl.BlockSpec((tm, LANES), lambda i: (tab_block_of(i), 0))
    outs = [
        jax.ShapeDtypeStruct((m, D_POOL), F32),
        jax.ShapeDtypeStruct((m, D_ATTN), BF16),
        jax.ShapeDtypeStruct((m, D_ATTN), F32),
        jax.ShapeDtypeStruct((m, D_ATTN), F32),
        jax.ShapeDtypeStruct((m, D_ATTN), BF16),
        jax.ShapeDtypeStruct((m, D_ATTN), BF16),
        jax.ShapeDtypeStruct((m, 2 * D_MODEL), F32),
    ]
    scratch = [pltpu.VMEM((tm, D_MODEL), BF16)]
    if tiles_per_seq is not None:
        outs.append(jax.ShapeDtypeStruct((m, D_POOL), BF16))
        scratch.append(pltpu.VMEM((HALO + tm, D_POOL), F32))
    return pl.pallas_call(
        functools.partial(_proj_kernel, tiles_per_seq=tiles_per_seq),
        grid=(m // tm,),
        in_specs=[pl.BlockSpec((tm, D_MODEL), row), _resident((1, D_MODEL)), _resident((D_MODEL, D_IN)), tab, tab],
        out_specs=tuple(pl.BlockSpec((tm, s.shape[1]), row) for s in outs),
        out_shape=tuple(outs),
        scratch_shapes=scratch,
        compiler_params=_params("parallel" if tiles_per_seq is None else "arbitrary"),
        name="proj",
    )(x2d, g_mix, w_in_b, cos_tab, sin_tab)


def _pool_sample_kernel(state_ref, u_ref, o_ref, *, past_len):
    for g, w in enumerate(POOL_WINDOWS):
        sl = slice(g * POOL_GROUP, (g + 1) * POOL_GROUP)
        cur = u_ref[:, sl]
        acc = cur
        for k in range(1, w):
            acc = acc + state_ref[POOL_BUF - k, :, sl]
        count = float(min(past_len + 1, w))
        o_ref[:, sl] = (acc / count - cur).astype(o_ref.dtype)


def _pool_sample(state_t, u_s, *, past_len):
    n = u_s.shape[0]
    return pl.pallas_call(
        functools.partial(_pool_sample_kernel, past_len=past_len),
        out_shape=jax.ShapeDtypeStruct((n, D_POOL), BF16),
        name="pool_sample",
    )(state_t, u_s)


def _subln(o, g, lam_init):
    ms = jnp.mean(o * o, axis=-1, keepdims=True)
    return o * lax.rsqrt(ms + SUBLN_EPS) * g * (1.0 - lam_init)


ONES_ROWS = 16
VT_ROWS = HEAD_W + ONES_ROWS
COL_STRIP = 256


def _attn_prompt_kernel(lam_ref, g_ref, q_ref, k_ref, v_ref, o_ref,
                        vt_scr, qt_scr, s0_scr, s1_scr, m_scr, acc_scr, *, blk, lam_init):
    nblk = k_ref.shape[0] // blk
    drow = lax.broadcasted_iota(jnp.int32, (HEAD_W, blk), 0)
    for r in range(nblk):
        rows = slice(r * blk, (r + 1) * blk)
        vt_scr[r, 0:HEAD_W, :] = v_ref[rows, :].astype(F32).T.astype(BF16)
        vt_scr[r, HEAD_W:VT_ROWS, :] = jnp.ones((ONES_ROWS, blk), BF16)
        qt = q_ref[rows, :].astype(F32).T
        qt_scr[r, 0] = jnp.where(drow < HEAD_DIM, qt, 0.0).astype(BF16)
        qt_scr[r, 1] = jnp.where(drow >= HEAD_DIM, qt, 0.0).astype(BF16)
    lam = lam_ref[0:1, 0:1]
    gain = g_ref[...] * (1.0 - lam_init)
    s_bufs = (s0_scr, s1_scr)
    units = [(qi, j) for qi in range(nblk) for j in range(qi + 1)]
    strips = [slice(h * COL_STRIP, (h + 1) * COL_STRIP) for h in range(blk // COL_STRIP)]

    def n_keys(qi, j, h):
        return (h + 1) * COL_STRIP if j == qi else blk

    def fill(u):
        qi, j = units[u]
        for c in range(2):
            if j != qi:
                s_bufs[u % 2][c] = jnp.dot(k_ref[j * blk:(j + 1) * blk, :], qt_scr[qi, c],
                                           preferred_element_type=F32)
                continue
            for h, cols in enumerate(strips):
                nk = n_keys(qi, j, h)
                s = jnp.dot(k_ref[j * blk:j * blk + nk, :], qt_scr[qi, c, :, cols], preferred_element_type=F32)
                krow = lax.broadcasted_iota(jnp.int32, (nk, COL_STRIP), 0)
                qcol = lax.broadcasted_iota(jnp.int32, (nk, COL_STRIP), 1) + h * COL_STRIP
                s_bufs[u % 2][c, 0:nk, cols] = jnp.where(krow <= qcol, s, NEG_INF)

    def consume(u):
        qi, j = units[u]
        st = qi % 2
        for c in range(2):
            for h, cols in enumerate(strips):
                nk = n_keys(qi, j, h)
                vt = vt_scr[j, :, 0:nk]
                s_ref = s_bufs[u % 2].at[c, 0:nk, cols]
                m_blk = jnp.max(s_ref[...], axis=0, keepdims=True)
                if j == 0:
                    m_new = m_blk
                    p = jnp.exp2(s_ref[...] - m_new).astype(BF16)
                    acc_scr[st, c, :, cols] = jnp.dot(vt, p, preferred_element_type=F32)
                else:
                    m_prev = m_scr[st, c, :, cols]
                    m_new = jnp.maximum(m_prev, m_blk)
                    p = jnp.exp2(s_ref[...] - m_new).astype(BF16)
                    acc_scr[st, c, :, cols] = (jnp.exp2(m_prev - m_new) * acc_scr[st, c, :, cols]
                                               + jnp.dot(vt, p, preferred_element_type=F32))
                m_scr[st, c, :, cols] = m_new

    def finalize(qi):
        a1 = acc_scr[qi % 2, 0]
        a2 = acc_scr[qi % 2, 1]
        ot = (a1[0:HEAD_W] * (1.0 / a1[HEAD_W:HEAD_W + 1])
              - lam * (a2[0:HEAD_W] * (1.0 / a2[HEAD_W:HEAD_W + 1])))
        ms = jnp.mean(ot * ot, axis=0, keepdims=True)
        on = (ot * lax.rsqrt(ms + SUBLN_EPS)).T * gain
        o_ref[qi * blk:(qi + 1) * blk, :] = on.astype(o_ref.dtype)

    fill(0)
    for u, (qi, j) in enumerate(units):
        if u + 1 < len(units):
            fill(u + 1)
        consume(u)
        if j == qi:
            finalize(qi)


def _attn_prompt(lam, g_subln, q, kb, vb, *, blk, lam_init):
    b, t, _ = q.shape
    kern = functools.partial(_attn_prompt_kernel, blk=blk, lam_init=lam_init)
    seq = pl.BlockSpec((None, t, HEAD_W), lambda bi, h: (bi, 0, h))
    return pl.pallas_call(
        kern,
        grid=(b, N_HEADS),
        in_specs=[_resident((SUBLANES, LANES)), _resident((1, HEAD_W)), seq, seq, seq],
        out_specs=seq,
        out_shape=jax.ShapeDtypeStruct((b, t, D_ATTN), BF16),
        scratch_shapes=[pltpu.VMEM((t // blk, VT_ROWS, blk), BF16), pltpu.VMEM((t // blk, 2, HEAD_W, blk), BF16),
                        pltpu.VMEM((2, blk, blk), F32), pltpu.VMEM((2, blk, blk), F32),
                        pltpu.VMEM((2, 2, 1, blk), F32), pltpu.VMEM((2, 2, VT_ROWS, blk), F32)],
        compiler_params=_params("parallel", "parallel"),
        name="attn_prompt",
    )(lam, g_subln, q, kb, vb)


N_MAPROWS = 2 * N_HEADS
HALF_TOKENS = PAGE_SIZE // 2
HALF_KEYS = HALF_TOKENS * N_HEADS
QUARTER_TOKENS = PAGE_SIZE // 4
QUARTER_KEYS = QUARTER_TOKENS * N_HEADS


def _decode_weights(q_ref, kn_ref, k_refs):
    row = lax.broadcasted_iota(jnp.int32, (N_MAPROWS, HEAD_W), 0)
    lane = lax.broadcasted_iota(jnp.int32, (N_MAPROWS, HEAD_W), 1)
    q = q_ref[...].astype(F32)
    qm = jnp.where((lane >= HEAD_DIM) == (row >= N_HEADS), jnp.concatenate([q, q], axis=0), 0.0)
    zero = jnp.zeros_like(qm)
    qm2 = jnp.concatenate([jnp.concatenate([qm, zero], axis=1),
                           jnp.concatenate([zero, qm], axis=1)], axis=0).astype(BF16)

    krow = lax.broadcasted_iota(jnp.int32, (2 * N_MAPROWS, HALF_KEYS), 0)
    kcol = lax.broadcasted_iota(jnp.int32, (2 * N_MAPROWS, HALF_KEYS), 1)
    head_bits = N_HEADS - 1
    own = (kcol & head_bits) == (krow & head_bits)

    def scores(queries, keys):
        return lax.dot_general(queries, keys, (((1,), (1,)), ((), ())), preferred_element_type=F32)

    s_pages = [jnp.where(own, scores(qm2, _page_halves(kr)), NEG_INF) for kr in k_refs]
    s_new = jnp.where(lane == (row & head_bits), scores(qm.astype(BF16), _pad_new(kn_ref)), NEG_INF)

    m2 = jnp.max(s_pages[0], axis=-1, keepdims=True)
    for s in s_pages[1:]:
        m2 = jnp.maximum(m2, jnp.max(s, axis=-1, keepdims=True))
    m = jnp.maximum(jnp.maximum(m2[0:N_MAPROWS], m2[N_MAPROWS:2 * N_MAPROWS]), jnp.max(s_new, axis=-1, keepdims=True))
    mm = jnp.concatenate([m, m], axis=0)
    p_new = jnp.exp2(s_new - m)
    l = jnp.sum(p_new, axis=-1, keepdims=True)
    l2 = jnp.zeros((2 * N_MAPROWS, 1), F32)
    p_pages = []
    for s in s_pages:
        p = jnp.exp2(s - mm)
        l2 = l2 + jnp.sum(p, axis=-1, keepdims=True)
        p_pages.append(p.astype(BF16))
    l = l + l2[0:N_MAPROWS] + l2[N_MAPROWS:2 * N_MAPROWS]
    return p_pages, p_new.astype(BF16), l


def _page_halves(ref):
    a = ref[0:HALF_TOKENS].reshape(HALF_KEYS, HEAD_W)
    b = ref[HALF_TOKENS:PAGE_SIZE].reshape(HALF_KEYS, HEAD_W)
    return jnp.concatenate([a, b], axis=1).astype(BF16)


def _pad_new(ref):
    pad = jnp.zeros((LANES - N_HEADS, HEAD_W), F32)
    return jnp.concatenate([ref[...], pad], axis=0).astype(BF16)


def _page_quarters(ref):
    parts = [ref[i * QUARTER_TOKENS:(i + 1) * QUARTER_TOKENS].reshape(QUARTER_KEYS, HEAD_W) for i in range(4)]
    return jnp.concatenate(parts, axis=1).astype(BF16)


def _decode_output(weights, lam_ref, g_ref, vn_ref, v_refs, o_ref, *, lam_init):
    p_pages, p_new, l = weights
    acc = jnp.dot(p_new, _pad_new(vn_ref), preferred_element_type=F32)
    acc4 = jnp.zeros((4 * N_MAPROWS, 4 * HEAD_W), F32)
    for p, vr in zip(p_pages, v_refs):
        p4 = jnp.concatenate([p[0:N_MAPROWS, 0:QUARTER_KEYS], p[0:N_MAPROWS, QUARTER_KEYS:HALF_KEYS],
                              p[N_MAPROWS:2 * N_MAPROWS, 0:QUARTER_KEYS],
                              p[N_MAPROWS:2 * N_MAPROWS, QUARTER_KEYS:HALF_KEYS]], axis=0)
        acc4 = acc4 + jnp.dot(p4, _page_quarters(vr), preferred_element_type=F32)
    for i in range(4):
        acc = acc + acc4[i * N_MAPROWS:(i + 1) * N_MAPROWS, i * HEAD_W:(i + 1) * HEAD_W]
    a = acc / l
    o = a[0:N_HEADS] - lam_ref[0:1, 0:1] * a[N_HEADS:N_MAPROWS]
    o_ref[...] = _subln(o, g_ref[...], lam_init).astype(o_ref.dtype)


def _mix_kernel(x_ref, pooled_ref, on_ref, gate_ref, wg_ref, scale_ref, wpp_ref, wap_ref, wo_ref, o_ref):
    parts = []
    for g in range(len(POOL_WINDOWS)):
        sl = slice(g * POOL_GROUP, (g + 1) * POOL_GROUP)
        parts.append(jnp.dot(pooled_ref[:, sl], wg_ref[g], preferred_element_type=F32))
    pool_out = jnp.concatenate(parts, axis=1) * scale_ref[...]
    pool_proj = jnp.dot(pool_out.astype(BF16), wpp_ref[...], preferred_element_type=F32)
    attn_proj = jnp.dot(on_ref[...].astype(BF16), wap_ref[...], preferred_element_type=F32)
    merged = gate_ref[:, 0:D_MODEL] * pool_proj + gate_ref[:, D_MODEL:2 * D_MODEL] * attn_proj
    o_ref[...] = x_ref[...] + jnp.dot(merged.astype(BF16), wo_ref[...], preferred_element_type=F32)


def _mix(x2d, pooled, on, gates, wg_b, pool_scale, wpp_b, wap_b, wo_b, *, tm):
    m = x2d.shape[0]
    row = lambda i: (i, 0)
    return pl.pallas_call(
        _mix_kernel,
        grid=(m // tm,),
        in_specs=[pl.BlockSpec((tm, D_MODEL), row), pl.BlockSpec((tm, D_POOL), row),
                  pl.BlockSpec((tm, D_ATTN), row), pl.BlockSpec((tm, 2 * D_MODEL), row),
                  _resident(wg_b.shape), _resident((1, D_POOL)), _resident(wpp_b.shape),
                  _resident(wap_b.shape), _resident(wo_b.shape)],
        out_specs=pl.BlockSpec((tm, D_MODEL), row),
        out_shape=jax.ShapeDtypeStruct((m, D_MODEL), F32),
        compiler_params=_params("parallel"),
        name="mix",
    )(x2d, pooled, on, gates, wg_b, pool_scale, wpp_b, wap_b, wo_b)


MLP_CHUNK = 2048


def _rms(x, g):
    ms = jnp.mean(x * x, axis=-1, keepdims=True)
    return x * lax.rsqrt(ms + RMS_EPS) * g


def _mlp_kernel(x_ref, g_ref, wup_ref, wdn_ref, gf_ref, o_ref, h_scr):
    x = x_ref[...]
    h_scr[...] = _rms(x, g_ref[...]).astype(BF16)
    y = x
    for c in range(D_FF // MLP_CHUNK):
        sl = slice(c * MLP_CHUNK, (c + 1) * MLP_CHUNK)
        a = jnp.maximum(jnp.dot(h_scr[...], wup_ref[:, sl], preferred_element_type=F32), 0.0)
        y = y + jnp.dot((a * a).astype(BF16), wdn_ref[sl, :], preferred_element_type=F32)
    o_ref[...] = _rms(y, gf_ref[...])


def _mlp(x2d, g_mlp, wup_b, wdn_b, g_final, *, tm):
    m = x2d.shape[0]
    row = lambda i: (i, 0)
    return pl.pallas_call(
        _mlp_kernel,
        grid=(m // tm,),
        in_specs=[pl.BlockSpec((tm, D_MODEL), row), _resident((1, D_MODEL)), _resident(wup_b.shape),
                  _resident(wdn_b.shape), _resident((1, D_MODEL))],
        out_specs=pl.BlockSpec((tm, D_MODEL), row),
        out_shape=jax.ShapeDtypeStruct((m, D_MODEL), F32),
        scratch_shapes=[pltpu.VMEM((tm, D_MODEL), BF16)],
        compiler_params=_params("parallel"),
        name="mlp",
    )(x2d, g_mlp, wup_b, wdn_b, g_final)


N_MLP_IN = 5
N_DEC_IN = 5


def _mlp_decode_kernel(pt_ref, *refs, n_pages, lam_init):
    del pt_ref
    mlp_in = refs[:N_MLP_IN]
    lam_ref, gs_ref, q_ref, kn_ref, vn_ref = refs[N_MLP_IN:N_MLP_IN + N_DEC_IN]
    pages = refs[N_MLP_IN + N_DEC_IN:N_MLP_IN + N_DEC_IN + 2 * n_pages]
    y_ref, on_ref, h_scr = refs[N_MLP_IN + N_DEC_IN + 2 * n_pages:]
    weights = _decode_weights(q_ref, kn_ref, pages[:n_pages])
    _mlp_kernel(*mlp_in, y_ref, h_scr)
    _decode_output(weights, lam_ref, gs_ref, vn_ref, pages[n_pages:], on_ref, lam_init=lam_init)


def _mlp_decode(x2d, g_mlp, wup_b, wdn_b, g_final, page_table, lam, g_subln, q3, kn3, vn3, cache_k4, cache_v4,
                *, lam_init):
    m = x2d.shape[0]
    n, n_pages = page_table.shape
    tm = m // n
    assert tm * n == m and tm % SUBLANES == 0
    pt_flat = page_table.reshape(-1)
    const2 = lambda i, pt: (0, 0)
    row = pl.BlockSpec((tm, D_MODEL), lambda i, pt: (i, 0))
    seq = pl.BlockSpec((None, N_HEADS, HEAD_W), lambda i, pt: (i, 0, 0))

    def whole(shape):
        return pl.BlockSpec(shape, const2, pipeline_mode=pl.Buffered(1))

    def page_spec(j):
        return pl.BlockSpec((None, PAGE_SIZE, N_HEADS, HEAD_W), lambda i, pt: (pt[i * n_pages + j], 0, 0, 0))

    grid_spec = pltpu.PrefetchScalarGridSpec(
        num_scalar_prefetch=1,
        grid=(n,),
        in_specs=[row, whole((1, D_MODEL)), whole(wup_b.shape), whole(wdn_b.shape), whole((1, D_MODEL)),
                  whole((SUBLANES, LANES)), whole((1, HEAD_W)), seq, seq, seq]
                 + [page_spec(j) for j in range(n_pages)] * 2,
        out_specs=(row, seq),
        scratch_shapes=[pltpu.VMEM((tm, D_MODEL), BF16)],
    )
    kern = functools.partial(_mlp_decode_kernel, n_pages=n_pages, lam_init=lam_init)
    return pl.pallas_call(
        kern,
        grid_spec=grid_spec,
        out_shape=(jax.ShapeDtypeStruct((m, D_MODEL), F32), jax.ShapeDtypeStruct((n, N_HEADS, HEAD_W), F32)),
        compiler_params=_params("parallel"),
        name="mlp_decode",
    )(pt_flat, x2d, g_mlp, wup_b, wdn_b, g_final, lam, g_subln, q3, kn3, vn3,
      *([cache_k4] * n_pages), *([cache_v4] * n_pages))


PROMPT_TM = 512
MIX_TM = 1024
ATTN_BLK = 512


def kernel(x_prompt, x_sample, state_pool, cache_k, cache_v, page_table, g_mix, w_in, w_pool_grp, pool_scale,
           w_pool_proj, lambda_q1, lambda_k1, lambda_q2, lambda_k2, g_subln, w_attn_proj, w_o, g_mlp, w_up,
           w_down, g_final):
    depth = w_in.shape[0]
    assert depth == 1, "single-layer trunk"
    layer = 0
    lam_init = _lambda_init(layer)
    b, t, _ = x_prompt.shape
    n, t_s, _ = x_sample.shape
    assert t_s == 1 and t % PROMPT_TM == 0 and t % ATTN_BLK == 0
    past_len = page_table.shape[1] * PAGE_SIZE

    w_in_b = w_in[layer].astype(BF16)
    wg_b = w_pool_grp[layer].astype(BF16)
    wpp_b = w_pool_proj[layer].astype(BF16)
    wap_b = w_attn_proj[layer].astype(BF16)
    wo_b = w_o[layer].astype(BF16)
    wup_b = w_up[layer].astype(BF16)
    wdn_b = w_down[layer].astype(BF16)
    g_mix2 = g_mix[layer][None, :]
    g_mlp2 = g_mlp[layer][None, :]
    g_fin2 = g_final[None, :]
    g_sub2 = g_subln[layer][None, :]
    scale2 = pool_scale[layer][None, :]

    cos_tab, sin_tab, lam = _prep(lambda_q1[layer], lambda_k1[layer], lambda_q2[layer], lambda_k2[layer],
                                  n_prompt=t, n_sample_rows=n, past_len=past_len, lam_init=lam_init)

    xs2 = x_sample.reshape(n, D_MODEL)
    u_s, q_s, k_s, v_s, _, _, gates_s = _proj(
        xs2, g_mix2, w_in_b, cos_tab, sin_tab, tm=n, tab_block_of=lambda i: t // n)
    tiles_per_seq = t // PROMPT_TM
    xp2 = x_prompt.reshape(b * t, D_MODEL)
    u_p, q_p, k_p, v_p, kb_p, vb_p, gates_p, pooled_p = _proj(
        xp2, g_mix2, w_in_b, cos_tab, sin_tab, tm=PROMPT_TM, tab_block_of=lambda i: i % tiles_per_seq,
        tiles_per_seq=tiles_per_seq)

    on_p = _attn_prompt(lam, g_sub2, q_p.reshape(b, t, D_ATTN), kb_p.reshape(b, t, D_ATTN),
                        vb_p.reshape(b, t, D_ATTN), blk=ATTN_BLK, lam_init=lam_init)
    x1_p = _mix(xp2, pooled_p, on_p.reshape(b * t, D_ATTN), gates_p, wg_b, scale2, wpp_b, wap_b, wo_b,
                tm=MIX_TM)

    heads = lambda a: a.reshape(n, N_HEADS, HEAD_W)
    y_p, on_s = _mlp_decode(x1_p, g_mlp2, wup_b, wdn_b, g_fin2, page_table, lam, g_sub2,
                            heads(q_s), heads(k_s), heads(v_s), cache_k[layer], cache_v[layer], lam_init=lam_init)

    state_t = jnp.swapaxes(state_pool[layer], 0, 1)
    pooled_s = _pool_sample(state_t, u_s, past_len=past_len)
    x1_s = _mix(xs2, pooled_s, on_s.reshape(n, D_ATTN), gates_s, wg_b, scale2, wpp_b, wap_b, wo_b, tm=n)
    y_s = _mlp(x1_s, g_mlp2, wup_b, wdn_b, g_fin2, tm=n)

    y_prompt = y_p.reshape(b, t, D_MODEL)
    y_sample = y_s.reshape(n, 1, D_MODEL)
    new_k_prompt = k_p.reshape(1, b, t, N_HEADS, HEAD_W)
    new_v_prompt = v_p.reshape(1, b, t, N_HEADS, HEAD_W)
    new_pool_prompt = u_p.reshape(b, t, D_POOL)[:, t - POOL_BUF:, :][None]
    new_k_sample = k_s.reshape(1, n, 1, N_HEADS, HEAD_W)
    new_v_sample = v_s.reshape(1, n, 1, N_HEADS, HEAD_W)
    new_pool_sample = jnp.concatenate([state_pool[layer][:, 1:, :], u_s[:, None, :]], axis=1)[None]
    return (y_prompt, y_sample, new_k_prompt, new_v_prompt, new_pool_prompt,
            new_k_sample, new_v_sample, new_pool_sample)
```

```python
import functools
import math

import jax
import jax.numpy as jnp
from jax import lax
from jax.experimental import pallas as pl
from jax.experimental.pallas import tpu as pltpu

D_MODEL = 1024
N_HEADS = 8
HEAD_DIM = 64
HEAD_W = 2 * HEAD_DIM
D_ATTN = N_HEADS * HEAD_W
POOL_WINDOWS = (2, 4, 8, 16)
D_POOL = D_MODEL // 2
POOL_GROUP = D_POOL // len(POOL_WINDOWS)
POOL_BUF = max(POOL_WINDOWS) - 1
HALO = 16
D_FF = 4 * D_MODEL
ROPE_THETA = 10000.0
RMS_EPS = 1e-6
SUBLN_EPS = 1e-5
NEG_INF = -1e30
PAGE_SIZE = 128
OFF_Q = D_POOL
OFF_K = OFF_Q + D_ATTN
OFF_V = OFF_K + D_ATTN
OFF_G = OFF_V + D_ATTN
D_IN = OFF_G + 2 * D_MODEL

LANES = 128
SUBLANES = 8
VMEM_LIMIT = 56 * 1024 * 1024

BF16 = jnp.bfloat16
F32 = jnp.float32


def _lambda_init(layer):
    return 0.8 - 0.6 * math.exp(-0.3 * layer)


def _resident(shape):
    nd = len(shape)
    return pl.BlockSpec(shape, lambda *_: (0,) * nd, pipeline_mode=pl.Buffered(1))


def _params(*sem):
    return pltpu.CompilerParams(dimension_semantics=sem, vmem_limit_bytes=VMEM_LIMIT)


def _prep_kernel(invf_ref, lq1_ref, lk1_ref, lq2_ref, lk2_ref, cos_ref, sin_ref, lam_ref, *, n_prompt, past_len, lam_init):
    rows = cos_ref.shape[0]
    row = lax.broadcasted_iota(jnp.int32, (rows, LANES), 0)
    lane = lax.broadcasted_iota(jnp.int32, (rows, LANES), 1)
    pos = jnp.where(row < n_prompt, row, past_len).astype(F32)
    ang = pos * invf_ref[...]
    cos_ref[...] = jnp.cos(ang)
    sin_ref[...] = jnp.where((lane % HEAD_DIM) < HEAD_DIM // 2, -jnp.sin(ang), jnp.sin(ang))
    d1 = jnp.sum(lq1_ref[...] * lk1_ref[...], axis=-1, keepdims=True)
    d2 = jnp.sum(lq2_ref[...] * lk2_ref[...], axis=-1, keepdims=True)
    lam = jnp.exp(d1) - jnp.exp(d2) + lam_init
    lam_ref[...] = jnp.broadcast_to(lam, lam_ref.shape)


def _prep(lq1, lk1, lq2, lk2, *, n_prompt, n_sample_rows, past_len, lam_init):
    half = HEAD_DIM // 2
    inv_freq = ROPE_THETA ** (-jnp.arange(half, dtype=F32) * (2.0 / HEAD_DIM))
    invf = jnp.tile(inv_freq, LANES // half)[None, :]
    rows = n_prompt + n_sample_rows
    kern = functools.partial(_prep_kernel, n_prompt=n_prompt, past_len=past_len, lam_init=lam_init)
    return pl.pallas_call(
        kern,
        out_shape=(jax.ShapeDtypeStruct((rows, LANES), F32),
                   jax.ShapeDtypeStruct((rows, LANES), F32),
                   jax.ShapeDtypeStruct((SUBLANES, LANES), F32)),
        name="prep",
    )(invf, lq1[None, :], lk1[None, :], lq2[None, :], lk2[None, :])


PROJ_CHUNK = 512
Q_SCALE = HEAD_DIM ** -0.5 * math.log2(math.e)


def _rope(z, cos, sin, first_half):
    swapped = jnp.where(first_half, pltpu.roll(z, LANES - HEAD_DIM // 2, 1), pltpu.roll(z, HEAD_DIM // 2, 1))
    return z * cos + swapped * sin


POOL_ROWS = 64


def _window_means(ext_scr, pos, o_ref, tm):
    for g, w in enumerate(POOL_WINDOWS):
        sl = slice(g * POOL_GROUP, (g + 1) * POOL_GROUP)
        for r0 in range(0, tm, POOL_ROWS):
            cur = ext_scr[HALO + r0:HALO + r0 + POOL_ROWS, sl]
            acc = cur
            for k in range(1, w):
                acc = acc + ext_scr[HALO + r0 - k:HALO + r0 - k + POOL_ROWS, sl]
            count = jnp.minimum(pos[r0:r0 + POOL_ROWS] + 1, w).astype(F32)
            o_ref[r0:r0 + POOL_ROWS, sl] = (acc / count - cur).astype(o_ref.dtype)


def _proj_kernel(x_ref, g_ref, w_ref, cos_ref, sin_ref,
                 u_ref, q_ref, k_ref, v_ref, kb_ref, vb_ref, *rest, tiles_per_seq):
    if tiles_per_seq is None:
        (h_scr,) = rest
    else:
        pooled_ref, h_scr, ext_scr = rest
    tm = x_ref.shape[0]
    if tiles_per_seq is not None:
        i = pl.program_id(0) % tiles_per_seq

        @pl.when(i == 0)
        def _():
            ext_scr[0:HALO, :] = jnp.zeros((HALO, D_POOL), F32)

        @pl.when(i > 0)
        def _():
            ext_scr[0:HALO, :] = ext_scr[tm:tm + HALO, :]

    x = x_ref[...]
    ms = jnp.mean(x * x, axis=-1, keepdims=True)
    h_scr[...] = (x * lax.rsqrt(ms + RMS_EPS) * g_ref[...]).astype(BF16)
    cos = cos_ref[...]
    sin = sin_ref[...]
    lane = lax.broadcasted_iota(jnp.int32, (tm, LANES), 1)
    first_half = (lane % HEAD_DIM) < HEAD_DIM // 2

    def dot(off, width):
        return jnp.dot(h_scr[...], w_ref[:, off:off + width], preferred_element_type=F32)

    u = dot(0, D_POOL)
    u_ref[...] = u
    if tiles_per_seq is not None:
        ext_scr[HALO:HALO + tm, :] = u
    for c in range(D_ATTN // PROJ_CHUNK):
        zq = dot(OFF_Q + c * PROJ_CHUNK, PROJ_CHUNK)
        zk = dot(OFF_K + c * PROJ_CHUNK, PROJ_CHUNK)
        for j in range(PROJ_CHUNK // LANES):
            sl = slice(j * LANES, (j + 1) * LANES)
            osl = slice(c * PROJ_CHUNK + j * LANES, c * PROJ_CHUNK + (j + 1) * LANES)
            q_ref[:, osl] = (_rope(zq[:, sl], cos, sin, first_half) * Q_SCALE).astype(BF16)
            kr = _rope(zk[:, sl], cos, sin, first_half)
            k_ref[:, osl] = kr
            kb_ref[:, osl] = kr.astype(BF16)
        zv = dot(OFF_V + c * PROJ_CHUNK, PROJ_CHUNK)
        csl = slice(c * PROJ_CHUNK, (c + 1) * PROJ_CHUNK)
        v_ref[:, csl] = zv
        vb_ref[:, csl] = zv.astype(BF16)
    if tiles_per_seq is not None:
        _window_means(ext_scr, i * tm + lax.broadcasted_iota(jnp.int32, (tm, POOL_GROUP), 0), pooled_ref, tm)


def _proj(x2d, g_mix, w_in_b, cos_tab, sin_tab, *, tm, tab_block_of, tiles_per_seq=None):
    m = x2d.shape[0]
    row = lambda i: (i, 0)
    tab = pl.BlockSpec((tm, LANES), lambda i: (tab_block_of(i), 0))
    outs = [
        jax.ShapeDtypeStruct((m, D_POOL), F32),
        jax.ShapeDtypeStruct((m, D_ATTN), BF16),
        jax.ShapeDtypeStruct((m, D_ATTN), F32),
        jax.ShapeDtypeStruct((m, D_ATTN), F32),
        jax.ShapeDtypeStruct((m, D_ATTN), BF16),
        jax.ShapeDtypeStruct((m, D_ATTN), BF16),
    ]
    scratch = [pltpu.VMEM((tm, D_MODEL), BF16)]
    if tiles_per_seq is not None:
        outs.append(jax.ShapeDtypeStruct((m, D_POOL), BF16))
        scratch.append(pltpu.VMEM((HALO + tm, D_POOL), F32))
    return pl.pallas_call(
        functools.partial(_proj_kernel, tiles_per_seq=tiles_per_seq),
        grid=(m // tm,),
        in_specs=[pl.BlockSpec((tm, D_MODEL), row), _resident((1, D_MODEL)), _resident((D_MODEL, D_IN)), tab, tab],
        out_specs=tuple(pl.BlockSpec((tm, s.shape[1]), row) for s in outs),
        out_shape=tuple(outs),
        scratch_shapes=scratch,
        compiler_params=_params("parallel" if tiles_per_seq is None else "arbitrary"),
        name="proj",
    )(x2d, g_mix, w_in_b, cos_tab, sin_tab)


def _pool_sample_kernel(state_ref, u_ref, o_ref, *, past_len):
    for g, w in enumerate(POOL_WINDOWS):
        sl = slice(g * POOL_GROUP, (g + 1) * POOL_GROUP)
        cur = u_ref[:, sl]
        acc = cur
        for k in range(1, w):
            acc = acc + state_ref[POOL_BUF - k, :, sl]
        count = float(min(past_len + 1, w))
        o_ref[:, sl] = (acc / count - cur).astype(o_ref.dtype)


def _pool_sample(state_t, u_s, *, past_len):
    n = u_s.shape[0]
    return pl.pallas_call(
        functools.partial(_pool_sample_kernel, past_len=past_len),
        out_shape=jax.ShapeDtypeStruct((n, D_POOL), BF16),
        name="pool_sample",
    )(state_t, u_s)


def _subln(o, g, lam_init):
    ms = jnp.mean(o * o, axis=-1, keepdims=True)
    return o * lax.rsqrt(ms + SUBLN_EPS) * g * (1.0 - lam_init)


ONES_ROWS = 16
VT_ROWS = HEAD_W + ONES_ROWS
COL_STRIP = 256


def _attn_prompt_kernel(lam_ref, g_ref, q_ref, k_ref, v_ref, o_ref,
                        vt_scr, qt_scr, s0_scr, s1_scr, m_scr, acc_scr, *, blk, lam_init):
    nblk = k_ref.shape[0] // blk
    drow = lax.broadcasted_iota(jnp.int32, (HEAD_W, blk), 0)
    for r in range(nblk):
        rows = slice(r * blk, (r + 1) * blk)
        vt_scr[r, 0:HEAD_W, :] = v_ref[rows, :].astype(F32).T.astype(BF16)
        vt_scr[r, HEAD_W:VT_ROWS, :] = jnp.ones((ONES_ROWS, blk), BF16)
        qt = q_ref[rows, :].astype(F32).T
        qt_scr[r, 0] = jnp.where(drow < HEAD_DIM, qt, 0.0).astype(BF16)
        qt_scr[r, 1] = jnp.where(drow >= HEAD_DIM, qt, 0.0).astype(BF16)
    lam = lam_ref[0:1, 0:1]
    gain = g_ref[...] * (1.0 - lam_init)
    s_bufs = (s0_scr, s1_scr)
    units = [(qi, j) for qi in range(nblk) for j in range(qi + 1)]
    strips = [slice(h * COL_STRIP, (h + 1) * COL_STRIP) for h in range(blk // COL_STRIP)]

    def n_keys(qi, j, h):
        return (h + 1) * COL_STRIP if j == qi else blk

    def fill(u):
        qi, j = units[u]
        for c in range(2):
            if j != qi:
                s_bufs[u % 2][c] = jnp.dot(k_ref[j * blk:(j + 1) * blk, :], qt_scr[qi, c],
                                           preferred_element_type=F32)
                continue
            for h, cols in enumerate(strips):
                nk = n_keys(qi, j, h)
                s = jnp.dot(k_ref[j * blk:j * blk + nk, :], qt_scr[qi, c, :, cols], preferred_element_type=F32)
                krow = lax.broadcasted_iota(jnp.int32, (nk, COL_STRIP), 0)
                qcol = lax.broadcasted_iota(jnp.int32, (nk, COL_STRIP), 1) + h * COL_STRIP
                s_bufs[u % 2][c, 0:nk, cols] = jnp.where(krow <= qcol, s, NEG_INF)

    def consume(u):
        qi, j = units[u]
        st = qi % 2
        for c in range(2):
            for h, cols in enumerate(strips):
                nk = n_keys(qi, j, h)
                vt = vt_scr[j, :, 0:nk]
                s_ref = s_bufs[u % 2].at[c, 0:nk, cols]
                m_blk = jnp.max(s_ref[...], axis=0, keepdims=True)
                if j == 0:
                    m_new = m_blk
                    p = jnp.exp2(s_ref[...] - m_new).astype(BF16)
                    acc_scr[st, c, :, cols] = jnp.dot(vt, p, preferred_element_type=F32)
                else:
                    m_prev = m_scr[st, c, :, cols]
                    m_new = jnp.maximum(m_prev, m_blk)
                    p = jnp.exp2(s_ref[...] - m_new).astype(BF16)
                    acc_scr[st, c, :, cols] = (jnp.exp2(m_prev - m_new) * acc_scr[st, c, :, cols]
                                               + jnp.dot(vt, p, preferred_element_type=F32))
                m_scr[st, c, :, cols] = m_new

    def finalize(qi):
        a1 = acc_scr[qi % 2, 0]
        a2 = acc_scr[qi % 2, 1]
        ot = (a1[0:HEAD_W] * (1.0 / a1[HEAD_W:HEAD_W + 1])
              - lam * (a2[0:HEAD_W] * (1.0 / a2[HEAD_W:HEAD_W + 1])))
        ms = jnp.mean(ot * ot, axis=0, keepdims=True)
        on = (ot * lax.rsqrt(ms + SUBLN_EPS)).T * gain
        o_ref[qi * blk:(qi + 1) * blk, :] = on.astype(o_ref.dtype)

    fill(0)
    for u, (qi, j) in enumerate(units):
        if u + 1 < len(units):
            fill(u + 1)
        consume(u)
        if j == qi:
            finalize(qi)


def _attn_prompt(lam, g_subln, q, kb, vb, *, blk, lam_init):
    b, t, _ = q.shape
    kern = functools.partial(_attn_prompt_kernel, blk=blk, lam_init=lam_init)
    seq = pl.BlockSpec((None, t, HEAD_W), lambda bi, h: (bi, 0, h))
    return pl.pallas_call(
        kern,
        grid=(b, N_HEADS),
        in_specs=[_resident((SUBLANES, LANES)), _resident((1, HEAD_W)), seq, seq, seq],
        out_specs=seq,
        out_shape=jax.ShapeDtypeStruct((b, t, D_ATTN), BF16),
        scratch_shapes=[pltpu.VMEM((t // blk, VT_ROWS, blk), BF16), pltpu.VMEM((t // blk, 2, HEAD_W, blk), BF16),
                        pltpu.VMEM((2, blk, blk), F32), pltpu.VMEM((2, blk, blk), F32),
                        pltpu.VMEM((2, 2, 1, blk), F32), pltpu.VMEM((2, 2, VT_ROWS, blk), F32)],
        compiler_params=_params("parallel", "parallel"),
        name="attn_prompt",
    )(lam, g_subln, q, kb, vb)


N_MAPROWS = 2 * N_HEADS
HALF_TOKENS = PAGE_SIZE // 2
HALF_KEYS = HALF_TOKENS * N_HEADS
QUARTER_TOKENS = PAGE_SIZE // 4
QUARTER_KEYS = QUARTER_TOKENS * N_HEADS


def _decode_weights(q_ref, kn_ref, k_refs):
    row = lax.broadcasted_iota(jnp.int32, (N_MAPROWS, HEAD_W), 0)
    lane = lax.broadcasted_iota(jnp.int32, (N_MAPROWS, HEAD_W), 1)
    q = q_ref[...].astype(F32)
    qm = jnp.where((lane >= HEAD_DIM) == (row >= N_HEADS), jnp.concatenate([q, q], axis=0), 0.0)
    zero = jnp.zeros_like(qm)
    qm2 = jnp.concatenate([jnp.concatenate([qm, zero], axis=1),
                           jnp.concatenate([zero, qm], axis=1)], axis=0).astype(BF16)

    krow = lax.broadcasted_iota(jnp.int32, (2 * N_MAPROWS, HALF_KEYS), 0)
    kcol = lax.broadcasted_iota(jnp.int32, (2 * N_MAPROWS, HALF_KEYS), 1)
    head_bits = N_HEADS - 1
    own = (kcol & head_bits) == (krow & head_bits)

    def scores(queries, keys):
        return lax.dot_general(queries, keys, (((1,), (1,)), ((), ())), preferred_element_type=F32)

    s_pages = [jnp.where(own, scores(qm2, _page_halves(kr)), NEG_INF) for kr in k_refs]
    s_new = jnp.where(lane == (row & head_bits), scores(qm.astype(BF16), _pad_new(kn_ref)), NEG_INF)

    m2 = jnp.max(s_pages[0], axis=-1, keepdims=True)
    for s in s_pages[1:]:
        m2 = jnp.maximum(m2, jnp.max(s, axis=-1, keepdims=True))
    m = jnp.maximum(jnp.maximum(m2[0:N_MAPROWS], m2[N_MAPROWS:2 * N_MAPROWS]), jnp.max(s_new, axis=-1, keepdims=True))
    mm = jnp.concatenate([m, m], axis=0)
    p_new = jnp.exp2(s_new - m)
    l = jnp.sum(p_new, axis=-1, keepdims=True)
    l2 = jnp.zeros((2 * N_MAPROWS, 1), F32)
    p_pages = []
    for s in s_pages:
        p = jnp.exp2(s - mm)
        l2 = l2 + jnp.sum(p, axis=-1, keepdims=True)
        p_pages.append(p.astype(BF16))
    l = l + l2[0:N_MAPROWS] + l2[N_MAPROWS:2 * N_MAPROWS]
    return p_pages, p_new.astype(BF16), l


def _page_halves(ref):
    a = ref[0:HALF_TOKENS].reshape(HALF_KEYS, HEAD_W)
    b = ref[HALF_TOKENS:PAGE_SIZE].reshape(HALF_KEYS, HEAD_W)
    return jnp.concatenate([a, b], axis=1).astype(BF16)


def _pad_new(ref):
    pad = jnp.zeros((LANES - N_HEADS, HEAD_W), F32)
    return jnp.concatenate([ref[...], pad], axis=0).astype(BF16)


def _page_quarters(ref):
    parts = [ref[i * QUARTER_TOKENS:(i + 1) * QUARTER_TOKENS].reshape(QUARTER_KEYS, HEAD_W) for i in range(4)]
    return jnp.concatenate(parts, axis=1).astype(BF16)


def _decode_output(weights, lam_ref, g_ref, vn_ref, v_refs, o_ref, *, lam_init):
    p_pages, p_new, l = weights
    acc = jnp.dot(p_new, _pad_new(vn_ref), preferred_element_type=F32)
    acc4 = jnp.zeros((4 * N_MAPROWS, 4 * HEAD_W), F32)
    for p, vr in zip(p_pages, v_refs):
        p4 = jnp.concatenate([p[0:N_MAPROWS, 0:QUARTER_KEYS], p[0:N_MAPROWS, QUARTER_KEYS:HALF_KEYS],
                              p[N_MAPROWS:2 * N_MAPROWS, 0:QUARTER_KEYS],
                              p[N_MAPROWS:2 * N_MAPROWS, QUARTER_KEYS:HALF_KEYS]], axis=0)
        acc4 = acc4 + jnp.dot(p4, _page_quarters(vr), preferred_element_type=F32)
    for i in range(4):
        acc = acc + acc4[i * N_MAPROWS:(i + 1) * N_MAPROWS, i * HEAD_W:(i + 1) * HEAD_W]
    a = acc / l
    o = a[0:N_HEADS] - lam_ref[0:1, 0:1] * a[N_HEADS:N_MAPROWS]
    o_ref[...] = _subln(o, g_ref[...], lam_init).astype(o_ref.dtype)


GATE_SLAB = 512
N_GATE_SLABS = 2 * D_MODEL // GATE_SLAB


def _mix_kernel(x_ref, gm_ref, pooled_ref, on_ref, *refs):
    wgate_refs = refs[:N_GATE_SLABS]
    wg_ref, scale_ref, wpp_ref, wap_ref, wo_ref, o_ref, h_scr = refs[N_GATE_SLABS:]
    x = x_ref[...]
    h_scr[...] = _rms(x, gm_ref[...]).astype(BF16)

    def gate(first):
        z = jnp.concatenate([jnp.dot(h_scr[...], wgate_refs[first + c][...], preferred_element_type=F32)
                             for c in range(N_GATE_SLABS // 2)], axis=1)
        return jax.nn.sigmoid(z)

    parts = []
    for g in range(len(POOL_WINDOWS)):
        sl = slice(g * POOL_GROUP, (g + 1) * POOL_GROUP)
        parts.append(jnp.dot(pooled_ref[:, sl], wg_ref[g], preferred_element_type=F32))
    pool_out = jnp.concatenate(parts, axis=1) * scale_ref[...]
    merged = gate(0) * jnp.dot(pool_out.astype(BF16), wpp_ref[...], preferred_element_type=F32)
    merged = merged + gate(N_GATE_SLABS // 2) * jnp.dot(on_ref[...].astype(BF16), wap_ref[...],
                                                        preferred_element_type=F32)
    o_ref[...] = x + jnp.dot(merged.astype(BF16), wo_ref[...], preferred_element_type=F32)


def _mix(x2d, g_mix, pooled, on, w_in_b, wg_b, pool_scale, wpp_b, wap_b, wo_b, *, tm):
    m = x2d.shape[0]
    row = lambda i: (i, 0)
    gate_slabs = [pl.BlockSpec((D_MODEL, GATE_SLAB), lambda i, c=c: (0, OFF_G // GATE_SLAB + c),
                               pipeline_mode=pl.Buffered(1)) for c in range(N_GATE_SLABS)]
    return pl.pallas_call(
        _mix_kernel,
        grid=(m // tm,),
        in_specs=[pl.BlockSpec((tm, D_MODEL), row), _resident((1, D_MODEL)), pl.BlockSpec((tm, D_POOL), row),
                  pl.BlockSpec((tm, D_ATTN), row)] + gate_slabs
                 + [_resident(wg_b.shape), _resident((1, D_POOL)), _resident(wpp_b.shape),
                    _resident(wap_b.shape), _resident(wo_b.shape)],
        out_specs=pl.BlockSpec((tm, D_MODEL), row),
        out_shape=jax.ShapeDtypeStruct((m, D_MODEL), F32),
        scratch_shapes=[pltpu.VMEM((tm, D_MODEL), BF16)],
        compiler_params=_params("parallel"),
        name="mix",
    )(x2d, g_mix, pooled, on, *([w_in_b] * N_GATE_SLABS), wg_b, pool_scale, wpp_b, wap_b, wo_b)


MLP_CHUNK = 2048


def _rms(x, g):
    ms = jnp.mean(x * x, axis=-1, keepdims=True)
    return x * lax.rsqrt(ms + RMS_EPS) * g


def _mlp_kernel(x_ref, g_ref, wup_ref, wdn_ref, gf_ref, o_ref, h_scr):
    x = x_ref[...]
    h_scr[...] = _rms(x, g_ref[...]).astype(BF16)
    y = x
    for c in range(D_FF // MLP_CHUNK):
        sl = slice(c * MLP_CHUNK, (c + 1) * MLP_CHUNK)
        a = jnp.maximum(jnp.dot(h_scr[...], wup_ref[:, sl], preferred_element_type=F32), 0.0)
        y = y + jnp.dot((a * a).astype(BF16), wdn_ref[sl, :], preferred_element_type=F32)
    o_ref[...] = _rms(y, gf_ref[...])


def _mlp(x2d, g_mlp, wup_b, wdn_b, g_final, *, tm):
    m = x2d.shape[0]
    row = lambda i: (i, 0)
    return pl.pallas_call(
        _mlp_kernel,
        grid=(m // tm,),
        in_specs=[pl.BlockSpec((tm, D_MODEL), row), _resident((1, D_MODEL)), _resident(wup_b.shape),
                  _resident(wdn_b.shape), _resident((1, D_MODEL))],
        out_specs=pl.BlockSpec((tm, D_MODEL), row),
        out_shape=jax.ShapeDtypeStruct((m, D_MODEL), F32),
        scratch_shapes=[pltpu.VMEM((tm, D_MODEL), BF16)],
        compiler_params=_params("parallel"),
        name="mlp",
    )(x2d, g_mlp, wup_b, wdn_b, g_final)


N_MLP_IN = 5
N_DEC_IN = 5


def _mlp_decode_kernel(pt_ref, *refs, n_pages, lam_init):
    del pt_ref
    mlp_in = refs[:N_MLP_IN]
    lam_ref, gs_ref, q_ref, kn_ref, vn_ref = refs[N_MLP_IN:N_MLP_IN + N_DEC_IN]
    pages = refs[N_MLP_IN + N_DEC_IN:N_MLP_IN + N_DEC_IN + 2 * n_pages]
    y_ref, on_ref, h_scr = refs[N_MLP_IN + N_DEC_IN + 2 * n_pages:]
    weights = _decode_weights(q_ref, kn_ref, pages[:n_pages])
    _mlp_kernel(*mlp_in, y_ref, h_scr)
    _decode_output(weights, lam_ref, gs_ref, vn_ref, pages[n_pages:], on_ref, lam_init=lam_init)


def _mlp_decode(x2d, g_mlp, wup_b, wdn_b, g_final, page_table, lam, g_subln, q3, kn3, vn3, cache_k4, cache_v4,
                *, lam_init):
    m = x2d.shape[0]
    n, n_pages = page_table.shape
    tm = m // n
    assert tm * n == m and tm % SUBLANES == 0
    pt_flat = page_table.reshape(-1)
    const2 = lambda i, pt: (0, 0)
    row = pl.BlockSpec((tm, D_MODEL), lambda i, pt: (i, 0))
    seq = pl.BlockSpec((None, N_HEADS, HEAD_W), lambda i, pt: (i, 0, 0))

    def whole(shape):
        return pl.BlockSpec(shape, const2, pipeline_mode=pl.Buffered(1))

    def page_spec(j):
        return pl.BlockSpec((None, PAGE_SIZE, N_HEADS, HEAD_W), lambda i, pt: (pt[i * n_pages + j], 0, 0, 0))

    grid_spec = pltpu.PrefetchScalarGridSpec(
        num_scalar_prefetch=1,
        grid=(n,),
        in_specs=[row, whole((1, D_MODEL)), whole(wup_b.shape), whole(wdn_b.shape), whole((1, D_MODEL)),
                  whole((SUBLANES, LANES)), whole((1, HEAD_W)), seq, seq, seq]
                 + [page_spec(j) for j in range(n_pages)] * 2,
        out_specs=(row, seq),
        scratch_shapes=[pltpu.VMEM((tm, D_MODEL), BF16)],
    )
    kern = functools.partial(_mlp_decode_kernel, n_pages=n_pages, lam_init=lam_init)
    return pl.pallas_call(
        kern,
        grid_spec=grid_spec,
        out_shape=(jax.ShapeDtypeStruct((m, D_MODEL), F32), jax.ShapeDtypeStruct((n, N_HEADS, HEAD_W), F32)),
        compiler_params=_params("parallel"),
        name="mlp_decode",
    )(pt_flat, x2d, g_mlp, wup_b, wdn_b, g_final, lam, g_subln, q3, kn3, vn3,
      *([cache_k4] * n_pages), *([cache_v4] * n_pages))


PROMPT_TM = 512
MIX_TM = 1024
ATTN_BLK = 512


def kernel(x_prompt, x_sample, state_pool, cache_k, cache_v, page_table, g_mix, w_in, w_pool_grp, pool_scale,
           w_pool_proj, lambda_q1, lambda_k1, lambda_q2, lambda_k2, g_subln, w_attn_proj, w_o, g_mlp, w_up,
           w_down, g_final):
    depth = w_in.shape[0]
    assert depth == 1, "single-layer trunk"
    layer = 0
    lam_init = _lambda_init(layer)
    b, t, _ = x_prompt.shape
    n, t_s, _ = x_sample.shape
    assert t_s == 1 and t % PROMPT_TM == 0 and t % ATTN_BLK == 0
    past_len = page_table.shape[1] * PAGE_SIZE

    w_in_b = w_in[layer].astype(BF16)
    wg_b = w_pool_grp[layer].astype(BF16)
    wpp_b = w_pool_proj[layer].astype(BF16)
    wap_b = w_attn_proj[layer].astype(BF16)
    wo_b = w_o[layer].astype(BF16)
    wup_b = w_up[layer].astype(BF16)
    wdn_b = w_down[layer].astype(BF16)
    g_mix2 = g_mix[layer][None, :]
    g_mlp2 = g_mlp[layer][None, :]
    g_fin2 = g_final[None, :]
    g_sub2 = g_subln[layer][None, :]
    scale2 = pool_scale[layer][None, :]

    cos_tab, sin_tab, lam = _prep(lambda_q1[layer], lambda_k1[layer], lambda_q2[layer], lambda_k2[layer],
                                  n_prompt=t, n_sample_rows=n, past_len=past_len, lam_init=lam_init)

    xs2 = x_sample.reshape(n, D_MODEL)
    u_s, q_s, k_s, v_s, _, _ = _proj(
        xs2, g_mix2, w_in_b, cos_tab, sin_tab, tm=n, tab_block_of=lambda i: t // n)
    tiles_per_seq = t // PROMPT_TM
    xp2 = x_prompt.reshape(b * t, D_MODEL)
    u_p, q_p, k_p, v_p, kb_p, vb_p, pooled_p = _proj(
        xp2, g_mix2, w_in_b, cos_tab, sin_tab, tm=PROMPT_TM, tab_block_of=lambda i: i % tiles_per_seq,
        tiles_per_seq=tiles_per_seq)

    on_p = _attn_prompt(lam, g_sub2, q_p.reshape(b, t, D_ATTN), kb_p.reshape(b, t, D_ATTN),
                        vb_p.reshape(b, t, D_ATTN), blk=ATTN_BLK, lam_init=lam_init)
    x1_p = _mix(xp2, g_mix2, pooled_p, on_p.reshape(b * t, D_ATTN), w_in_b, wg_b, scale2, wpp_b, wap_b, wo_b,
                tm=MIX_TM)

    heads = lambda a: a.reshape(n, N_HEADS, HEAD_W)
    y_p, on_s = _mlp_decode(x1_p, g_mlp2, wup_b, wdn_b, g_fin2, page_table, lam, g_sub2,
                            heads(q_s), heads(k_s), heads(v_s), cache_k[layer], cache_v[layer], lam_init=lam_init)

    state_t = jnp.swapaxes(state_pool[layer], 0, 1)
    pooled_s = _pool_sample(state_t, u_s, past_len=past_len)
    x1_s = _mix(xs2, g_mix2, pooled_s, on_s.reshape(n, D_ATTN), w_in_b, wg_b, scale2, wpp_b, wap_b, wo_b, tm=n)
    y_s = _mlp(x1_s, g_mlp2, wup_b, wdn_b, g_fin2, tm=n)

    y_prompt = y_p.reshape(b, t, D_MODEL)
    y_sample = y_s.reshape(n, 1, D_MODEL)
    new_k_prompt = k_p.reshape(1, b, t, N_HEADS, HEAD_W)
    new_v_prompt = v_p.reshape(1, b, t, N_HEADS, HEAD_W)
    new_pool_prompt = u_p.reshape(b, t, D_POOL)[:, t - POOL_BUF:, :][None]
    new_k_sample = k_s.reshape(1, n, 1, N_HEADS, HEAD_W)
    new_v_sample = v_s.reshape(1, n, 1, N_HEADS, HEAD_W)
    new_pool_sample = jnp.concatenate([state_pool[layer][:, 1:, :], u_s[:, None, :]], axis=1)[None]
    return (y_prompt, y_sample, new_k_prompt, new_v_prompt, new_pool_prompt,
            new_k_sample, new_v_sample, new_pool_sample)
```
